```python
import jax, jax.numpy as jnp
from jax import lax
import numpy as np

D_MODEL = 2048
BATCH = 4
SEQ = 2048
DEPTH = 4
DEC_BATCH = 8
DEC_SEQ = 32
PAST_LEN = 4096

CHUNK = 64
GMLP_CHUNK = 128
GMLP_GROUPS = 8
D_A = D_MODEL
GMLP_GDIM = D_A // GMLP_GROUPS
HEAD_DIM = 128
N_HEADS = D_MODEL // HEAD_DIM
N_KV_HEADS = 4
Q_PER_KV = N_HEADS // N_KV_HEADS
D_B = N_HEADS * HEAD_DIM
KV_WIDTH = N_KV_HEADS * HEAD_DIM
IDX_HEADS = 16
IDX_DIM = 128
TOPK_MAX = 256
QBLOCK = 128
D_FF = 5632
ALPHA = (2 * DEPTH) ** 0.25
BETA = (8 * DEPTH) ** -0.25
LN_EPS = 1e-5
IDX_SCALE = (IDX_HEADS * IDX_DIM) ** -0.5
ATTN_SCALE = HEAD_DIM ** -0.5
IN_SIZES = (2 * D_A, D_B, KV_WIDTH, KV_WIDTH, IDX_HEADS * IDX_DIM, IDX_DIM, IDX_HEADS, 2 * D_MODEL)
IN_WIDTH = sum(IN_SIZES)

kernel_name = 'gmlp_dsa_macaron_deepnorm_stream_step'


def layer_norm(x, g, b):
    xf = x.astype(jnp.float32)
    mu = jnp.mean(xf, axis=-1, keepdims=True)
    xc = xf - mu
    var = jnp.mean(xc * xc, axis=-1, keepdims=True)
    y = xc * lax.rsqrt(var + LN_EPS)
    return (y * g.astype(jnp.float32) + b.astype(jnp.float32)).astype(x.dtype)


def modulate(x, shift, scale):
    return x * (1 + scale) + shift


def swiglu_ffn(h, w_up, w_down):
    a, b = jnp.split(h @ w_up, 2, axis=-1)
    return (jax.nn.silu(a) * b) @ w_down


def spatial_gate(v, w_s, b_s):
    c = v.shape[2]
    mask = jnp.tril(jnp.ones((c, c), dtype=bool))
    w = jnp.where(mask[None], w_s[:, :c, :c], jnp.zeros((), w_s.dtype))
    return jnp.einsum('gts,bnsgc->bntgc', w, v) + jnp.transpose(b_s[:, :c])[None, None, :, :, None]


def dsa_attend(q, qi, wi, k_all, v_all, ki_all, q_pos, k_pos, n_sel):
    f32 = jnp.float32
    admissible = (k_pos[None, :] // CHUNK) <= (q_pos[:, None] // CHUNK)
    logits = jnp.einsum('bqhd,bld->bqhl', qi.astype(f32), ki_all.astype(f32))
    score = jnp.einsum('bqhl,bqh->bql', jax.nn.relu(logits), wi.astype(f32)) * IDX_SCALE
    score = jnp.where(admissible[None], score, -jnp.inf)
    _, idx = lax.top_k(score, n_sel)
    valid = (k_pos[idx] // CHUNK) <= (q_pos[None, :, None] // CHUNK)
    gather = jax.vmap(lambda rows, ids: rows[ids])
    k_sel = gather(k_all, idx).astype(f32)
    v_sel = gather(v_all, idx).astype(f32)
    s = jnp.einsum('bqhgd,bqkhd->bqhgk', q.astype(f32), k_sel) * ATTN_SCALE
    s = jnp.where(valid[:, :, None, None, :], s, -jnp.inf)
    p = jax.nn.softmax(s, axis=-1)
    o = jnp.einsum('bqhgk,bqkhd->bqhgd', p, v_sel)
    return o.astype(q.dtype)


def token_mixer(h, past_k, past_v, past_ki, w_in, sgu_g, sgu_b, w_s, b_s, ki_g, ki_b, w_pa, w_pb, w_o):
    bsz, t, _ = h.shape
    points = [int(p) for p in np.cumsum(IN_SIZES)[:-1]]
    uv, q, k, v, qi, ki, wi, gates = jnp.split(h @ w_in, points, axis=-1)

    u, sv = jnp.split(jax.nn.gelu(uv), 2, axis=-1)
    sv = layer_norm(sv, sgu_g, sgu_b)
    if past_k is None:
        n_chunks, c_len = t // GMLP_CHUNK, GMLP_CHUNK
    else:
        n_chunks, c_len = 1, t
    f = spatial_gate(sv.reshape(bsz, n_chunks, c_len, GMLP_GROUPS, GMLP_GDIM), w_s, b_s)
    out_a = u * f.reshape(bsz, t, D_A)

    q = q.reshape(bsz, t, N_KV_HEADS, Q_PER_KV, HEAD_DIM)
    k = k.reshape(bsz, t, N_KV_HEADS, HEAD_DIM)
    v = v.reshape(bsz, t, N_KV_HEADS, HEAD_DIM)
    qi = qi.reshape(bsz, t, IDX_HEADS, IDX_DIM)
    ki = layer_norm(ki, ki_g, ki_b)
    if past_k is None:
        k_all, v_all, ki_all, start = k, v, ki, 0
    else:
        k_all = jnp.concatenate([past_k, k], axis=1)
        v_all = jnp.concatenate([past_v, v], axis=1)
        ki_all = jnp.concatenate([past_ki, ki], axis=1)
        start = past_k.shape[1]
    n_keys = k_all.shape[1]
    n_sel = min(TOPK_MAX, n_keys // 4)
    k_pos = jnp.arange(n_keys, dtype=jnp.int32)
    q_pos = start + jnp.arange(t, dtype=jnp.int32)
    if t % QBLOCK == 0:
        nb = t // QBLOCK
        blocks = lambda a: jnp.moveaxis(a.reshape((bsz, nb, QBLOCK) + a.shape[2:]), 1, 0)
        o = lax.map(lambda xs: dsa_attend(xs[0], xs[1], xs[2], k_all, v_all, ki_all, xs[3], k_pos, n_sel),
                    (blocks(q), blocks(qi), blocks(wi), q_pos.reshape(nb, QBLOCK)))
        o = jnp.moveaxis(o, 0, 1)
    else:
        o = dsa_attend(q, qi, wi, k_all, v_all, ki_all, q_pos, k_pos, n_sel)
    out_b = o.reshape(bsz, t, D_B)

    g_a, g_b = jnp.split(jax.nn.sigmoid(gates), 2, axis=-1)
    y = (g_a * (out_a @ w_pa) + g_b * (out_b @ w_pb)) @ w_o
    return y, k, v, ki, sv


def run_trunk(x, c, cache_k, cache_v, cache_kidx, w_ada, b_ada, w_ffn_up, w_ffn_down, w_in,
              sgu_ln_g, sgu_ln_b, w_spatial, b_spatial, kidx_ln_g, kidx_ln_b,
              w_branch_a, w_branch_b, w_out, ln_g, ln_b):
    new_k, new_v, new_ki, new_sv = [], [], [], []
    for l in range(DEPTH):
        mods = (jax.nn.silu(c) @ w_ada[l] + b_ada[l]).reshape(c.shape[0], 3, 3, 1, D_MODEL)
        h = modulate(x, mods[:, 0, 0], mods[:, 0, 1])
        x = layer_norm(ALPHA * x + 0.5 * mods[:, 0, 2] * swiglu_ffn(h, w_ffn_up[l, 0], w_ffn_down[l, 0]),
                       ln_g[l, 0], ln_b[l, 0])
        h = modulate(x, mods[:, 1, 0], mods[:, 1, 1])
        if cache_k is None:
            past_k, past_v, past_ki = None, None, None
        else:
            past_k, past_v, past_ki = cache_k[l], cache_v[l], cache_kidx[l]
        y, k, v, ki, sv = token_mixer(h, past_k, past_v, past_ki, w_in[l], sgu_ln_g[l], sgu_ln_b[l],
                                      w_spatial[l], b_spatial[l], kidx_ln_g[l], kidx_ln_b[l],
                                      w_branch_a[l], w_branch_b[l], w_out[l])
        x = layer_norm(ALPHA * x + mods[:, 1, 2] * y, ln_g[l, 1], ln_b[l, 1])
        h = modulate(x, mods[:, 2, 0], mods[:, 2, 1])
        x = layer_norm(ALPHA * x + 0.5 * mods[:, 2, 2] * swiglu_ffn(h, w_ffn_up[l, 1], w_ffn_down[l, 1]),
                       ln_g[l, 2], ln_b[l, 2])
        new_k.append(k)
        new_v.append(v)
        new_ki.append(ki)
        new_sv.append(sv)
    return x, new_k, new_v, new_ki, new_sv


def setup_inputs(seed: int = 0) -> dict:
    key = jax.random.key(seed)
    ks = jax.random.split(key, 23)
    d = D_MODEL

    def nrm(k, shape, scale):
        return jax.random.normal(k, shape, jnp.float32) * scale

    return {
        'x_prompt': nrm(ks[0], (BATCH, SEQ, d), 1.0),
        'x_sample': nrm(ks[1], (DEC_BATCH, DEC_SEQ, d), 1.0),
        'c_prompt': nrm(ks[2], (BATCH, d), 1.0),
        'c_sample': nrm(ks[3], (DEC_BATCH, d), 1.0),
        'cache_k': nrm(ks[4], (DEPTH, DEC_BATCH, PAST_LEN, N_KV_HEADS, HEAD_DIM), 1.0),
        'cache_v': nrm(ks[5], (DEPTH, DEC_BATCH, PAST_LEN, N_KV_HEADS, HEAD_DIM), 1.0),
        'cache_kidx': nrm(ks[6], (DEPTH, DEC_BATCH, PAST_LEN, IDX_DIM), 1.0),
        'w_ada': nrm(ks[7], (DEPTH, d, 9 * d), 0.5 * d ** -0.5),
        'b_ada': nrm(ks[8], (DEPTH, 9 * d), 0.01),
        'w_ffn_up': nrm(ks[9], (DEPTH, 2, d, 2 * D_FF), d ** -0.5),
        'w_ffn_down': nrm(ks[10], (DEPTH, 2, D_FF, d), BETA * D_FF ** -0.5),
        'w_in': nrm(ks[11], (DEPTH, d, IN_WIDTH), d ** -0.5),
        'sgu_ln_g': 1.0 + nrm(ks[12], (DEPTH, D_A), 0.02),
        'sgu_ln_b': nrm(ks[13], (DEPTH, D_A), 0.02),
        'w_spatial': nrm(ks[14], (DEPTH, GMLP_GROUPS, GMLP_CHUNK, GMLP_CHUNK), GMLP_CHUNK ** -0.5),
        'b_spatial': 1.0 + nrm(ks[15], (DEPTH, GMLP_GROUPS, GMLP_CHUNK), 0.02),
        'kidx_ln_g': 1.0 + nrm(ks[16], (DEPTH, IDX_DIM), 0.02),
        'kidx_ln_b': nrm(ks[17], (DEPTH, IDX_DIM), 0.02),
        'w_branch_a': nrm(ks[18], (DEPTH, D_A, d), D_A ** -0.5),
        'w_branch_b': nrm(ks[19], (DEPTH, D_B, d), D_B ** -0.5),
        'w_out': nrm(ks[20], (DEPTH, d, d), BETA * d ** -0.5),
        'ln_g': 1.0 + nrm(ks[21], (DEPTH, 3, d), 0.02),
        'ln_b': nrm(ks[22], (DEPTH, 3, d), 0.02),
    }


def reference(x_prompt, x_sample, c_prompt, c_sample, cache_k, cache_v, cache_kidx, w_ada, b_ada,
              w_ffn_up, w_ffn_down, w_in, sgu_ln_g, sgu_ln_b, w_spatial, b_spatial, kidx_ln_g, kidx_ln_b,
              w_branch_a, w_branch_b, w_out, ln_g, ln_b):
    y_prompt, k_p, v_p, ki_p, _ = run_trunk(
        x_prompt, c_prompt, None, None, None, w_ada, b_ada, w_ffn_up, w_ffn_down, w_in,
        sgu_ln_g, sgu_ln_b, w_spatial, b_spatial, kidx_ln_g, kidx_ln_b,
        w_branch_a, w_branch_b, w_out, ln_g, ln_b)
    y_sample, k_s, v_s, ki_s, sv_s = run_trunk(
        x_sample, c_sample, cache_k, cache_v, cache_kidx, w_ada, b_ada, w_ffn_up, w_ffn_down, w_in,
        sgu_ln_g, sgu_ln_b, w_spatial, b_spatial, kidx_ln_g, kidx_ln_b,
        w_branch_a, w_branch_b, w_out, ln_g, ln_b)
    return (y_prompt, y_sample, jnp.stack(k_p), jnp.stack(v_p), jnp.stack(ki_p),
            jnp.stack(k_s), jnp.stack(v_s), jnp.stack(ki_s), jnp.stack(sv_s))
```

```python
import functools

import jax
import jax.numpy as jnp
from jax import lax
from jax.experimental import pallas as pl
from jax.experimental.pallas import tpu as pltpu

F32 = jnp.float32
BF16 = jnp.bfloat16

D_MODEL = 2048
BATCH = 4
SEQ = 2048
DEPTH = 4
DEC_BATCH = 8
DEC_SEQ = 32
PAST_LEN = 4096
CHUNK_SHIFT = 6
GMLP_CHUNK = 128
GMLP_GROUPS = 8
D_A = D_MODEL
GMLP_GDIM = D_A // GMLP_GROUPS
HEAD_DIM = 128
N_HEADS = D_MODEL // HEAD_DIM
N_KV_HEADS = 4
Q_PER_KV = N_HEADS // N_KV_HEADS
D_B = N_HEADS * HEAD_DIM
KV_WIDTH = N_KV_HEADS * HEAD_DIM
IDX_HEADS = 16
IDX_DIM = 128
TOPK_MAX = 256
D_FF = 5632
ALPHA = (2 * DEPTH) ** 0.25
LN_EPS = 1e-5
IDX_SCALE = (IDX_HEADS * IDX_DIM) ** -0.5
ATTN_SCALE = HEAD_DIM ** -0.5
ADA_WIDTH = 9 * D_MODEL
ADA_ROWS = 16

COL_U = 0
COL_SV = D_A
COL_Q = 2 * D_A
COL_QI = COL_Q + D_B
COL_GA = COL_QI + IDX_HEADS * IDX_DIM
COL_GB = COL_GA + D_MODEL
COL_K = COL_GB + D_MODEL
COL_V = COL_K + KV_WIDTH
COL_KI = COL_V + KV_WIDTH
COL_WI = COL_KI + IDX_DIM
IN_PAD_WIDTH = 13824

LANE = 128
VMEM_LIMIT = 56 * 1024 * 1024

NT_DIMS = (((1,), (1,)), ((), ()))


def _params(*sem):
    return pltpu.CompilerParams(dimension_semantics=sem, vmem_limit_bytes=VMEM_LIMIT)


def _layer_norm_rows(r, g, b):
    mu = jnp.mean(r, axis=-1, keepdims=True)
    rc = r - mu
    var = jnp.mean(rc * rc, axis=-1, keepdims=True)
    return rc * lax.rsqrt(var + LN_EPS) * g + b


def _ada_kernel(c_ref, w_ref, b_ref, o_ref):
    c = c_ref[...]
    s = (c * jax.nn.sigmoid(c)).astype(BF16)
    o_ref[0] = jnp.dot(s, w_ref[0].astype(BF16), preferred_element_type=F32) + b_ref[0]


def _ada(c_all, w_ada, b_ada):
    tn = 1024
    return pl.pallas_call(
        _ada_kernel,
        out_shape=jax.ShapeDtypeStruct((DEPTH, ADA_ROWS, ADA_WIDTH), F32),
        grid=(DEPTH, ADA_WIDTH // tn),
        in_specs=[
            pl.BlockSpec((ADA_ROWS, D_MODEL), lambda l, j: (0, 0)),
            pl.BlockSpec((1, D_MODEL, tn), lambda l, j: (l, 0, j)),
            pl.BlockSpec((1, 1, tn), lambda l, j: (l, 0, j)),
        ],
        out_specs=pl.BlockSpec((1, ADA_ROWS, tn), lambda l, j: (l, 0, j)),
        compiler_params=_params("parallel", "parallel"),
        name="ada",
    )(c_all, w_ada, b_ada.reshape(DEPTH, 1, ADA_WIDTH))


def _mod_kernel(x_ref, sh_ref, sc_ref, h_ref):
    h_ref[...] = (x_ref[...] * (1.0 + sc_ref[0]) + sh_ref[0]).astype(BF16)


def _mod_spec(mod, tm, tiles_per_group, n_lead):
    r = mod.shape[1]
    if n_lead == 1:
        return pl.BlockSpec((1, r, D_MODEL), lambda i: (i // tiles_per_group, 0, 0))
    return pl.BlockSpec((1, r, D_MODEL), lambda i, k: (i // tiles_per_group, 0, 0))


def _modulate(x, shift, scale, tm, tpg):
    m = x.shape[0]
    return pl.pallas_call(
        _mod_kernel,
        out_shape=jax.ShapeDtypeStruct((m, D_MODEL), BF16),
        grid=(m // tm,),
        in_specs=[pl.BlockSpec((tm, D_MODEL), lambda i: (i, 0)),
                  _mod_spec(shift, tm, tpg, 1), _mod_spec(scale, tm, tpg, 1)],
        out_specs=pl.BlockSpec((tm, D_MODEL), lambda i: (i, 0)),
        compiler_params=_params("parallel"),
        name="modulate",
    )(x, shift, scale)


def _up_kernel(h_ref, wa_ref, wb_ref, o_ref):
    h = h_ref[...]
    a = jnp.dot(h, wa_ref[...], preferred_element_type=F32)
    b = jnp.dot(h, wb_ref[...], preferred_element_type=F32)
    o_ref[...] = (a * jax.nn.sigmoid(a) * b).astype(BF16)


def _ffn_up(h, w_up, tm):
    m = h.shape[0]
    tf = 512
    nf = D_FF // tf
    return pl.pallas_call(
        _up_kernel,
        out_shape=jax.ShapeDtypeStruct((m, D_FF), BF16),
        grid=(nf, m // tm),
        in_specs=[
            pl.BlockSpec((tm, D_MODEL), lambda j, i: (i, 0)),
            pl.BlockSpec((D_MODEL, tf), lambda j, i: (0, j)),
            pl.BlockSpec((D_MODEL, tf), lambda j, i: (0, j + nf)),
        ],
        out_specs=pl.BlockSpec((tm, tf), lambda j, i: (i, j)),
        compiler_params=_params("parallel", "parallel"),
        name="ffn_up",
    )(h, w_up, w_up)


def _down_kernel(*refs, gscale, nk, emit_h):
    if emit_h:
        a_ref, w_ref, x_ref, g_ref, lg_ref, lb_ref, sh_ref, sc_ref, xo_ref, ho_ref, acc_ref = refs
    else:
        a_ref, w_ref, x_ref, g_ref, lg_ref, lb_ref, xo_ref, acc_ref = refs
    k = pl.program_id(1)

    @pl.when(k == 0)
    def _():
        acc_ref[...] = jnp.zeros_like(acc_ref)

    acc_ref[...] += jnp.dot(a_ref[...], w_ref[...], preferred_element_type=F32)

    @pl.when(k == nk - 1)
    def _():
        r = ALPHA * x_ref[...] + (gscale * g_ref[0]) * acc_ref[...]
        xn = _layer_norm_rows(r, lg_ref[...], lb_ref[...])
        xo_ref[...] = xn
        if emit_h:
            ho_ref[...] = (xn * (1.0 + sc_ref[0]) + sh_ref[0]).astype(BF16)


def _proj_resid_ln(a, w, x, gate, lg, lb, nxt, gscale, tm, tpg):
    m, kdim = a.shape
    tk = 512
    nk = kdim // tk
    emit_h = nxt is not None
    row = lambda i, k: (i, 0)
    in_specs = [
        pl.BlockSpec((tm, tk), lambda i, k: (i, k)),
        pl.BlockSpec((tk, D_MODEL), lambda i, k: (k, 0)),
        pl.BlockSpec((tm, D_MODEL), row),
        _mod_spec(gate, tm, tpg, 2),
        pl.BlockSpec((1, D_MODEL), lambda i, k: (0, 0)),
        pl.BlockSpec((1, D_MODEL), lambda i, k: (0, 0)),
    ]
    args = [a, w, x, gate, lg, lb]
    out_shape = [jax.ShapeDtypeStruct((m, D_MODEL), F32)]
    out_specs = [pl.BlockSpec((tm, D_MODEL), row)]
    if emit_h:
        in_specs += [_mod_spec(nxt[0], tm, tpg, 2), _mod_spec(nxt[1], tm, tpg, 2)]
        args += [nxt[0], nxt[1]]
        out_shape.append(jax.ShapeDtypeStruct((m, D_MODEL), BF16))
        out_specs.append(pl.BlockSpec((tm, D_MODEL), row))
    outs = pl.pallas_call(
        functools.partial(_down_kernel, gscale=gscale, nk=nk, emit_h=emit_h),
        out_shape=out_shape,
        grid=(m // tm, nk),
        in_specs=in_specs,
        out_specs=out_specs,
        scratch_shapes=[pltpu.VMEM((tm, D_MODEL), F32)],
        compiler_params=_params("parallel", "arbitrary"),
        name="proj_resid_ln",
    )(*args)
    return (outs[0], outs[1]) if emit_h else (outs[0], None)


def _gemm_kernel(h_ref, w_ref, o_ref):
    o_ref[...] = jnp.dot(h_ref[...], w_ref[...], preferred_element_type=F32)


def _in_proj(h, w, tm):
    m = h.shape[0]
    tn = 1536
    return pl.pallas_call(
        _gemm_kernel,
        out_shape=jax.ShapeDtypeStruct((m, IN_PAD_WIDTH), F32),
        grid=(IN_PAD_WIDTH // tn, m // tm),
        in_specs=[pl.BlockSpec((tm, D_MODEL), lambda j, i: (i, 0)),
                  pl.BlockSpec((D_MODEL, tn), lambda j, i: (0, j))],
        out_specs=pl.BlockSpec((tm, tn), lambda j, i: (i, j)),
        compiler_params=_params("parallel", "parallel"),
        name="in_proj",
    )(h, w)


def _gmlp_kernel(u_ref, sv_ref, k_ref, v_ref, ki_ref, sg_ref, sb_ref, ws_ref, bias_ref, kg_ref, kb_ref,
                 oa_ref, kbf_ref, vbf_ref, kiln_ref, kibf_ref, svn_ref, *, tr, chunk):
    u = jax.nn.gelu(u_ref[...])
    svn = _layer_norm_rows(jax.nn.gelu(sv_ref[...]), sg_ref[...], sb_ref[...])
    svn_ref[...] = svn
    svb = svn.astype(BF16)
    row = lax.broadcasted_iota(jnp.int32, (tr, tr), 0)
    col = lax.broadcasted_iota(jnp.int32, (tr, tr), 1)
    shift = chunk.bit_length() - 1
    visible = ((row >> shift) == (col >> shift)) & (col <= row)
    for g in range(GMLP_GROUPS):
        cols = slice(g * GMLP_GDIM, (g + 1) * GMLP_GDIM)
        wm = jnp.where(visible, ws_ref[g], 0.0).astype(BF16)
        f = jnp.dot(wm, svb[:, cols], preferred_element_type=F32) + bias_ref[:, cols]
        oa_ref[:, cols] = (u[:, cols] * f).astype(BF16)
    kbf_ref[...] = k_ref[...].astype(BF16)
    vbf_ref[...] = v_ref[...].astype(BF16)
    kiln = _layer_norm_rows(ki_ref[...], kg_ref[...], kb_ref[...])
    kiln_ref[...] = kiln
    kibf_ref[...] = kiln.astype(BF16)


def _gmlp(p, sgu_g, sgu_b, w_tiled, bias_tiled, ki_g, ki_b, chunk, tr):
    m = p.shape[0]
    row = lambda c: (lambda i: (i, c))
    const2 = lambda i: (0, 0)
    return pl.pallas_call(
        functools.partial(_gmlp_kernel, tr=tr, chunk=chunk),
        out_shape=[
            jax.ShapeDtypeStruct((m, D_A), BF16),
            jax.ShapeDtypeStruct((m, KV_WIDTH), BF16),
            jax.ShapeDtypeStruct((m, KV_WIDTH), BF16),
            jax.ShapeDtypeStruct((m, IDX_DIM), F32),
            jax.ShapeDtypeStruct((m, IDX_DIM), BF16),
            jax.ShapeDtypeStruct((m, D_A), F32),
        ],
        grid=(m // tr,),
        in_specs=[
            pl.BlockSpec((tr, D_A), row(COL_U // D_A)),
            pl.BlockSpec((tr, D_A), row(COL_SV // D_A)),
            pl.BlockSpec((tr, KV_WIDTH), row(COL_K // KV_WIDTH)),
            pl.BlockSpec((tr, KV_WIDTH), row(COL_V // KV_WIDTH)),
            pl.BlockSpec((tr, IDX_DIM), row(COL_KI // IDX_DIM)),
            pl.BlockSpec((1, D_A), const2),
            pl.BlockSpec((1, D_A), const2),
            pl.BlockSpec((GMLP_GROUPS, tr, tr), lambda i: (0, 0, 0)),
            pl.BlockSpec((tr, D_A), const2),
            pl.BlockSpec((1, IDX_DIM), const2),
            pl.BlockSpec((1, IDX_DIM), const2),
        ],
        out_specs=[
            pl.BlockSpec((tr, D_A), row(0)),
            pl.BlockSpec((tr, KV_WIDTH), row(0)),
            pl.BlockSpec((tr, KV_WIDTH), row(0)),
            pl.BlockSpec((tr, IDX_DIM), row(0)),
            pl.BlockSpec((tr, IDX_DIM), row(0)),
            pl.BlockSpec((tr, D_A), row(0)),
        ],
        compiler_params=_params("parallel"),
        name="gmlp",
    )(p, p, p, p, p, sgu_g, sgu_b, w_tiled, bias_tiled, ki_g, ki_b)


def _key_to_float(key):
    return lax.bitcast_convert_type(key ^ ((key >> 31) & 0x7FFFFFFF), F32)


def _dsa_kernel(q_ref, qi_ref, wi_ref, k_ref, v_ref, ki_ref, o_ref, score_ref, *, tq, n_pad, n_keys, q_start, n_sel):
    i = pl.program_id(1)

    kib = ki_ref[...]
    wi = wi_ref[...]
    score = jnp.zeros((tq, n_pad), F32)
    for h in range(IDX_HEADS):
        qh = qi_ref[:, h * IDX_DIM:(h + 1) * IDX_DIM].astype(BF16)
        logits = lax.dot_general(qh, kib, NT_DIMS, preferred_element_type=F32)
        score = score + jnp.maximum(logits, 0.0) * wi[:, h:h + 1]
    score = score * IDX_SCALE
    qpos = q_start + i * tq + lax.broadcasted_iota(jnp.int32, (tq, n_pad), 0)
    kpos = lax.broadcasted_iota(jnp.int32, (tq, n_pad), 1)
    admissible = (kpos >> CHUNK_SHIFT) <= (qpos >> CHUNK_SHIFT)
    if n_keys < n_pad:
        admissible = admissible & (kpos < n_keys)
    score = jnp.where(admissible, score, -jnp.inf)
    score_ref[...] = score
    n_adm = jnp.sum(jnp.where(admissible, 1.0, 0.0), axis=-1, keepdims=True)

    def count_ge(key):
        return jnp.sum(jnp.where(score_ref[...] >= _key_to_float(key), 1.0, 0.0), axis=-1, keepdims=True)

    zero = jnp.zeros((tq, 1), jnp.int32)
    prefix = jnp.where(count_ge(zero) >= n_sel, zero, jnp.int32(-2 ** 31))
    for bit in range(30, -1, -1):
        cand = prefix | jnp.int32(1 << bit)
        prefix = jnp.where(count_ge(cand) >= n_sel, cand, prefix)
    tau = jnp.where(n_adm <= n_sel, -jnp.inf, _key_to_float(prefix))
    score = score_ref[...]
    bias = jnp.where(admissible, jnp.where(score >= tau, 0.0, -jnp.inf), -jnp.inf)

    for hk in range(N_KV_HEADS):
        heads = [hk * Q_PER_KV + g for g in range(Q_PER_KV)]
        qs = jnp.concatenate([q_ref[:, h * HEAD_DIM:(h + 1) * HEAD_DIM] for h in heads], axis=0).astype(BF16)
        kh = k_ref[:, hk * HEAD_DIM:(hk + 1) * HEAD_DIM]
        s = lax.dot_general(qs, kh, NT_DIMS, preferred_element_type=F32) * ATTN_SCALE
        ps, ls = [], []
        for g in range(Q_PER_KV):
            sg = s[g * tq:(g + 1) * tq] + bias
            pg = jnp.exp(sg - jnp.max(sg, axis=-1, keepdims=True))
            ls.append(jnp.sum(pg, axis=-1, keepdims=True))
            ps.append(pg.astype(BF16))
        o = jnp.dot(jnp.concatenate(ps, axis=0), v_ref[:, hk * HEAD_DIM:(hk + 1) * HEAD_DIM],
                    preferred_element_type=F32)
        for g, h in enumerate(heads):
            o_ref[:, h * HEAD_DIM:(h + 1) * HEAD_DIM] = (o[g * tq:(g + 1) * tq] / ls[g]).astype(BF16)


def _dsa(p, kbf, vbf, kibf, nb, t, tq, n_pad, n_keys, q_start):
    nq = t // tq
    n_sel = min(TOPK_MAX, n_keys // 4)
    qrow = lambda c: (lambda b, i: (b * nq + i, c))
    keys = lambda b, i: (b, 0)
    return pl.pallas_call(
        functools.partial(_dsa_kernel, tq=tq, n_pad=n_pad, n_keys=n_keys, q_start=q_start, n_sel=n_sel),
        out_shape=jax.ShapeDtypeStruct((nb * t, D_B), BF16),
        grid=(nb, nq),
        in_specs=[
            pl.BlockSpec((tq, D_B), qrow(COL_Q // D_B)),
            pl.BlockSpec((tq, IDX_HEADS * IDX_DIM), qrow(COL_QI // (IDX_HEADS * IDX_DIM))),
            pl.BlockSpec((tq, LANE), qrow(COL_WI // LANE)),
            pl.BlockSpec((n_pad, KV_WIDTH), keys),
            pl.BlockSpec((n_pad, KV_WIDTH), keys),
            pl.BlockSpec((n_pad, IDX_DIM), keys),
        ],
        out_specs=pl.BlockSpec((tq, D_B), lambda b, i: (b * nq + i, 0)),
        scratch_shapes=[pltpu.VMEM((tq, n_pad), F32)],
        compiler_params=_params("parallel", "parallel"),
        name="dsa",
    )(p, p, p, kbf, vbf, kibf)


def _merge_kernel(a_ref, b_ref, wa_ref, wb_ref, ga_ref, gb_ref, o_ref):
    ya = jnp.dot(a_ref[...], wa_ref[...], preferred_element_type=F32)
    yb = jnp.dot(b_ref[...], wb_ref[...], preferred_element_type=F32)
    o_ref[...] = (jax.nn.sigmoid(ga_ref[...]) * ya + jax.nn.sigmoid(gb_ref[...]) * yb).astype(BF16)


def _merge(out_a, out_b, w_pa, w_pb, p, tm):
    m = out_a.shape[0]
    tn = 512
    wspec = pl.BlockSpec((D_MODEL, tn), lambda j, i: (0, j))
    aspec = pl.BlockSpec((tm, D_MODEL), lambda j, i: (i, 0))
    return pl.pallas_call(
        _merge_kernel,
        out_shape=jax.ShapeDtypeStruct((m, D_MODEL), BF16),
        grid=(D_MODEL // tn, m // tm),
        in_specs=[aspec, aspec, wspec, wspec,
                  pl.BlockSpec((tm, tn), lambda j, i: (i, COL_GA // tn + j)),
                  pl.BlockSpec((tm, tn), lambda j, i: (i, COL_GB // tn + j))],
        out_specs=pl.BlockSpec((tm, tn), lambda j, i: (i, j)),
        compiler_params=_params("parallel", "parallel"),
        name="merge",
    )(out_a, out_b, w_pa, w_pb, p, p)


def _run_trunk(x, mods, weights, nb, t, tm, tr, chunk, caches):
    tpg = max(t // tm, 1)
    h = _modulate(x, mods(0, 0, 0), mods(0, 0, 1), tm, tpg)
    new_k, new_v, new_ki, new_sv = [], [], [], []
    for l in range(DEPTH):
        wl = weights[l]
        act = _ffn_up(h, wl["up"][0], tm)
        x, h = _proj_resid_ln(act, wl["down"][0], x, mods(l, 0, 2), wl["ln_g"][0], wl["ln_b"][0],
                              (mods(l, 1, 0), mods(l, 1, 1)), 0.5, tm, tpg)
        p = _in_proj(h, wl["in"], tm)
        out_a, kbf, vbf, kiln, kibf, svn = _gmlp(p, wl["sgu_g"], wl["sgu_b"], wl["ws"][chunk], wl["bs"][chunk],
                                                 wl["ki_g"], wl["ki_b"], chunk, tr)
        if caches is None:
            n_keys = n_pad = t
            q_start, tq = 0, 128
        else:
            past_k, past_v, past_ki = caches[l]
            n_keys = PAST_LEN + t
            n_pad = -(-n_keys // LANE) * LANE
            q_start, tq = PAST_LEN, t

            def cat(past, new):
                w = new.shape[-1]
                return jnp.concatenate([past, new.reshape(nb, t, w), jnp.zeros((nb, n_pad - n_keys, w), BF16)],
                                       axis=1).reshape(nb * n_pad, w)

            kbf, vbf, kibf = cat(past_k, kbf), cat(past_v, vbf), cat(past_ki, kibf)
        out_b = _dsa(p, kbf, vbf, kibf, nb, t, tq, n_pad, n_keys, q_start)
        z = _merge(out_a, out_b, wl["pa"], wl["pb"], p, tm)
        x, h = _proj_resid_ln(z, wl["out"], x, mods(l, 1, 2), wl["ln_g"][1], wl["ln_b"][1],
                              (mods(l, 2, 0), mods(l, 2, 1)), 1.0, tm, tpg)
        act = _ffn_up(h, wl["up"][1], tm)
        nxt = (mods(l + 1, 0, 0), mods(l + 1, 0, 1)) if l + 1 < DEPTH else None
        x, h = _proj_resid_ln(act, wl["down"][1], x, mods(l, 2, 2), wl["ln_g"][2], wl["ln_b"][2],
                              nxt, 0.5, tm, tpg)
        new_k.append(p[:, COL_K:COL_K + KV_WIDTH].reshape(nb, t, N_KV_HEADS, HEAD_DIM))
        new_v.append(p[:, COL_V:COL_V + KV_WIDTH].reshape(nb, t, N_KV_HEADS, HEAD_DIM))
        new_ki.append(kiln.reshape(nb, t, IDX_DIM))
        new_sv.append(svn.reshape(nb, t, D_A))
    return x, jnp.stack(new_k), jnp.stack(new_v), jnp.stack(new_ki), jnp.stack(new_sv)


def _reorder_in_proj(w_in):
    o_q, o_k, o_v = 2 * D_A, 2 * D_A + D_B, 2 * D_A + D_B + KV_WIDTH
    o_qi = o_v + KV_WIDTH
    o_ki = o_qi + IDX_HEADS * IDX_DIM
    o_wi = o_ki + IDX_DIM
    o_g = o_wi + IDX_HEADS
    seg = lambda a, n: w_in[:, :, a:a + n]
    parts = [seg(0, 2 * D_A), seg(o_q, D_B), seg(o_qi, IDX_HEADS * IDX_DIM), seg(o_g, 2 * D_MODEL),
             seg(o_k, KV_WIDTH), seg(o_v, KV_WIDTH), seg(o_ki, IDX_DIM), seg(o_wi, IDX_HEADS)]
    used = sum(q.shape[-1] for q in parts)
    parts.append(jnp.zeros((DEPTH, D_MODEL, IN_PAD_WIDTH - used), w_in.dtype))
    return jnp.concatenate(parts, axis=-1).astype(BF16)


def kernel(x_prompt, x_sample, c_prompt, c_sample, cache_k, cache_v, cache_kidx, w_ada, b_ada, w_ffn_up,
           w_ffn_down, w_in, sgu_ln_g, sgu_ln_b, w_spatial, b_spatial, kidx_ln_g, kidx_ln_b, w_branch_a,
           w_branch_b, w_out, ln_g, ln_b):
    tr = 256
    c_all = jnp.concatenate([c_prompt, c_sample, jnp.zeros((ADA_ROWS - BATCH - DEC_BATCH, D_MODEL), F32)], axis=0)
    ada = _ada(c_all, w_ada, b_ada)

    def mod_cols(l, sub, kind, rows):
        c0 = (sub * 3 + kind) * D_MODEL
        return ada[l, rows, c0:c0 + D_MODEL]

    def mods_prompt(l, sub, kind):
        return mod_cols(l, sub, kind, slice(0, BATCH))[:, None, :]

    def mods_sample(l, sub, kind):
        rows = mod_cols(l, sub, kind, slice(BATCH, BATCH + DEC_BATCH))
        return jnp.repeat(rows, DEC_SEQ, axis=0)[None]

    w_up = w_ffn_up.astype(BF16)
    w_down = w_ffn_down.astype(BF16)
    w_inr = _reorder_in_proj(w_in)
    w_pa, w_pb, w_o = w_branch_a.astype(BF16), w_branch_b.astype(BF16), w_out.astype(BF16)

    def spatial(l, chunk):
        reps = tr // chunk
        w = jnp.tile(w_spatial[l][:, :chunk, :chunk], (1, reps, reps))
        b = jnp.tile(jnp.repeat(jnp.transpose(b_spatial[l][:, :chunk]), GMLP_GDIM, axis=1), (reps, 1))
        return w, b

    weights = []
    for l in range(DEPTH):
        sp = {c: spatial(l, c) for c in (GMLP_CHUNK, DEC_SEQ)}
        weights.append(dict(
            up=w_up[l], down=w_down[l], **{"in": w_inr[l]}, pa=w_pa[l], pb=w_pb[l], out=w_o[l],
            ln_g=ln_g[l][:, None, :], ln_b=ln_b[l][:, None, :],
            sgu_g=sgu_ln_g[l][None], sgu_b=sgu_ln_b[l][None], ki_g=kidx_ln_g[l][None], ki_b=kidx_ln_b[l][None],
            ws={c: sp[c][0] for c in sp}, bs={c: sp[c][1] for c in sp}))

    y_p, k_p, v_p, ki_p, _ = _run_trunk(x_prompt.reshape(BATCH * SEQ, D_MODEL), mods_prompt, weights,
                                        BATCH, SEQ, 512, tr, GMLP_CHUNK, None)
    caches = [(cache_k[l].reshape(DEC_BATCH, PAST_LEN, KV_WIDTH).astype(BF16),
               cache_v[l].reshape(DEC_BATCH, PAST_LEN, KV_WIDTH).astype(BF16),
               cache_kidx[l].astype(BF16)) for l in range(DEPTH)]
    m_s = DEC_BATCH * DEC_SEQ
    y_s, k_s, v_s, ki_s, sv_s = _run_trunk(x_sample.reshape(m_s, D_MODEL), mods_sample, weights,
                                           DEC_BATCH, DEC_SEQ, m_s, tr, DEC_SEQ, caches)
    return (y_p.reshape(BATCH, SEQ, D_MODEL), y_s.reshape(DEC_BATCH, DEC_SEQ, D_MODEL),
            k_p, v_p, ki_p, k_s, v_s, ki_s, sv_s)
```

```python
import functools

import jax
import jax.numpy as jnp
from jax import lax
from jax.experimental import pallas as pl
from jax.experimental.pallas import tpu as pltpu

F32 = jnp.float32
BF16 = jnp.bfloat16

D_MODEL = 2048
BATCH = 4
SEQ = 2048
DEPTH = 4
DEC_BATCH = 8
DEC_SEQ = 32
PAST_LEN = 4096
CHUNK_SHIFT = 6
GMLP_CHUNK = 128
GMLP_GROUPS = 8
D_A = D_MODEL
GMLP_GDIM = D_A // GMLP_GROUPS
HEAD_DIM = 128
N_HEADS = D_MODEL // HEAD_DIM
N_KV_HEADS = 4
Q_PER_KV = N_HEADS // N_KV_HEADS
D_B = N_HEADS * HEAD_DIM
KV_WIDTH = N_KV_HEADS * HEAD_DIM
IDX_HEADS = 16
IDX_DIM = 128
IDX_WIDTH = IDX_HEADS * IDX_DIM
TOPK_MAX = 256
D_FF = 5632
ALPHA = (2 * DEPTH) ** 0.25
LN_EPS = 1e-5
IDX_SCALE = (IDX_HEADS * IDX_DIM) ** -0.5
ATTN_SCALE = HEAD_DIM ** -0.5
ADA_WIDTH = 9 * D_MODEL
ADA_ROWS = 16

COL_U = 0
COL_SV = D_A
COL_Q = 2 * D_A
COL_K = COL_Q + D_B
COL_V = COL_K + KV_WIDTH
COL_QI = COL_V + KV_WIDTH
COL_KI = COL_QI + IDX_WIDTH
COL_WI = COL_KI + IDX_DIM
COL_GATES = COL_WI + IDX_HEADS
IN_TN = 1024
IN_MAIN_WIDTH = 10 * IN_TN

LANE = 128
VMEM_LIMIT = 56 * 1024 * 1024
NT_DIMS = (((1,), (1,)), ((), ()))


def _params(*sem):
    return pltpu.CompilerParams(dimension_semantics=sem, vmem_limit_bytes=VMEM_LIMIT)


def _layer_norm_rows(r, g, b):
    mu = jnp.mean(r, axis=-1, keepdims=True)
    rc = r - mu
    var = jnp.mean(rc * rc, axis=-1, keepdims=True)
    return rc * lax.rsqrt(var + LN_EPS) * g + b


def _key_to_float(key):
    return lax.bitcast_convert_type(key ^ ((key >> 31) & 0x7FFFFFFF), F32)


def _kth_largest_key(count_ge, shape, n_sel):
    zero = jnp.zeros(shape, jnp.int32)
    prefix = jnp.where(count_ge(zero) >= n_sel, zero, jnp.int32(-2 ** 31))
    for bit in range(30, -1, -1):
        cand = prefix | jnp.int32(1 << bit)
        prefix = jnp.where(count_ge(cand) >= n_sel, cand, prefix)
    return prefix


def _ada_kernel(c_ref, w_ref, b_ref, o_ref):
    c = c_ref[...]
    s = (c * jax.nn.sigmoid(c)).astype(BF16)
    o_ref[0] = jnp.dot(s, w_ref[0].astype(BF16), preferred_element_type=F32) + b_ref[0]


def _ada(c_all, w_ada, b_ada):
    tn = 1024
    return pl.pallas_call(
        _ada_kernel,
        out_shape=jax.ShapeDtypeStruct((DEPTH, ADA_ROWS, ADA_WIDTH), F32),
        grid=(DEPTH, ADA_WIDTH // tn),
        in_specs=[
            pl.BlockSpec((ADA_ROWS, D_MODEL), lambda l, j: (0, 0)),
            pl.BlockSpec((1, D_MODEL, tn), lambda l, j: (l, 0, j)),
            pl.BlockSpec((1, 1, tn), lambda l, j: (l, 0, j)),
        ],
        out_specs=pl.BlockSpec((1, ADA_ROWS, tn), lambda l, j: (l, 0, j)),
        compiler_params=_params("parallel", "parallel"),
        name="ada",
    )(c_all, w_ada, b_ada.reshape(DEPTH, 1, ADA_WIDTH))


def _mod_kernel(x_ref, sh_ref, sc_ref, h_ref):
    h_ref[...] = (x_ref[...] * (1.0 + sc_ref[0]) + sh_ref[0]).astype(BF16)


def _mod_spec(mod, tiles_per_group):
    return pl.BlockSpec((1, mod.shape[1], D_MODEL), lambda i, *_: (i // tiles_per_group, 0, 0))


def _modulate(x, shift, scale, tm, tpg):
    m = x.shape[0]
    return pl.pallas_call(
        _mod_kernel,
        out_shape=jax.ShapeDtypeStruct((m, D_MODEL), BF16),
        grid=(m // tm,),
        in_specs=[pl.BlockSpec((tm, D_MODEL), lambda i: (i, 0)), _mod_spec(shift, tpg), _mod_spec(scale, tpg)],
        out_specs=pl.BlockSpec((tm, D_MODEL), lambda i: (i, 0)),
        compiler_params=_params("parallel"),
        name="modulate",
    )(x, shift, scale)


def _up_kernel(h_ref, wa_ref, wb_ref, o_ref, wa_s, wb_s):
    @pl.when(pl.program_id(1) == 0)
    def _():
        wa_s[...] = wa_ref[0, 0].astype(BF16)
        wb_s[...] = wb_ref[0, 0].astype(BF16)

    h = h_ref[...]
    a = jnp.dot(h, wa_s[...], preferred_element_type=F32)
    b = jnp.dot(h, wb_s[...], preferred_element_type=F32)
    o_ref[...] = (a * jax.nn.sigmoid(a) * b).astype(BF16)


def _ffn_up(h, w_up, l, s, tm):
    m = h.shape[0]
    tf = 512
    nf = D_FF // tf
    return pl.pallas_call(
        _up_kernel,
        out_shape=jax.ShapeDtypeStruct((m, D_FF), BF16),
        grid=(nf, m // tm),
        in_specs=[
            pl.BlockSpec((tm, D_MODEL), lambda j, i: (i, 0)),
            pl.BlockSpec((1, 1, D_MODEL, tf), lambda j, i: (l, s, 0, j)),
            pl.BlockSpec((1, 1, D_MODEL, tf), lambda j, i: (l, s, 0, j + nf)),
        ],
        out_specs=pl.BlockSpec((tm, tf), lambda j, i: (i, j)),
        scratch_shapes=[pltpu.VMEM((D_MODEL, tf), BF16), pltpu.VMEM((D_MODEL, tf), BF16)],
        compiler_params=_params("parallel", "arbitrary"),
        name="ffn_up",
    )(h, w_up, w_up)


def _down_kernel(*refs, gscale, nk, emit_h):
    if emit_h:
        a_ref, w_ref, x_ref, g_ref, lg_ref, lb_ref, sh_ref, sc_ref, xo_ref, ho_ref, acc_ref = refs
    else:
        a_ref, w_ref, x_ref, g_ref, lg_ref, lb_ref, xo_ref, acc_ref = refs
    k = pl.program_id(1)
    part = jnp.dot(a_ref[...], w_ref[0, 0], preferred_element_type=F32)

    def finish(y):
        r = ALPHA * x_ref[...] + (gscale * g_ref[0]) * y
        xn = _layer_norm_rows(r, lg_ref[...], lb_ref[...])
        xo_ref[...] = xn
        if emit_h:
            ho_ref[...] = (xn * (1.0 + sc_ref[0]) + sh_ref[0]).astype(BF16)

    if nk == 1:
        finish(part)
        return

    @pl.when(k == 0)
    def _():
        acc_ref[...] = part

    @pl.when((k > 0) & (k < nk - 1))
    def _():
        acc_ref[...] += part

    @pl.when(k == nk - 1)
    def _():
        finish(acc_ref[...] + part)


def _proj_resid_ln(a, w, l, s, x, gate, lg, lb, nxt, gscale, tm, tpg):
    m, kdim = a.shape
    tk = 1408 if kdim == D_FF else kdim
    nk = kdim // tk
    emit_h = nxt is not None
    row = lambda i, k: (i, 0)
    const = lambda i, k: (0, 0)
    in_specs = [
        pl.BlockSpec((tm, tk), lambda i, k: (i, k)),
        pl.BlockSpec((1, 1, tk, D_MODEL), lambda i, k: (l, s, k, 0)),
        pl.BlockSpec((tm, D_MODEL), row),
        _mod_spec(gate, tpg),
        pl.BlockSpec((1, D_MODEL), const),
        pl.BlockSpec((1, D_MODEL), const),
    ]
    args = [a, w, x, gate, lg, lb]
    out_shape = [jax.ShapeDtypeStruct((m, D_MODEL), F32)]
    out_specs = [pl.BlockSpec((tm, D_MODEL), row)]
    if emit_h:
        in_specs += [_mod_spec(nxt[0], tpg), _mod_spec(nxt[1], tpg)]
        args += [nxt[0], nxt[1]]
        out_shape.append(jax.ShapeDtypeStruct((m, D_MODEL), BF16))
        out_specs.append(pl.BlockSpec((tm, D_MODEL), row))
    outs = pl.pallas_call(
        functools.partial(_down_kernel, gscale=gscale, nk=nk, emit_h=emit_h),
        out_shape=out_shape,
        grid=(m // tm, nk),
        in_specs=in_specs,
        out_specs=out_specs,
        scratch_shapes=[pltpu.VMEM((tm, D_MODEL) if nk > 1 else (8, LANE), F32)],
        compiler_params=_params("parallel", "arbitrary"),
        name="proj_resid_ln",
    )(*args)
    return (outs[0], outs[1]) if emit_h else (outs[0], None)


def _in_proj_kernel(h_ref, w_ref, o_ref, w_s):
    @pl.when(pl.program_id(1) == 0)
    def _():
        w_s[...] = w_ref[0].astype(BF16)

    o_ref[...] = jnp.dot(h_ref[...], w_s[...], preferred_element_type=F32)


def _in_proj(h, w_in, l, tm):
    m = h.shape[0]
    return pl.pallas_call(
        _in_proj_kernel,
        out_shape=jax.ShapeDtypeStruct((m, IN_MAIN_WIDTH), F32),
        grid=(IN_MAIN_WIDTH // IN_TN, m // tm),
        in_specs=[pl.BlockSpec((tm, D_MODEL), lambda j, i: (i, 0)),
                  pl.BlockSpec((1, D_MODEL, IN_TN), lambda j, i: (l, 0, j))],
        out_specs=pl.BlockSpec((tm, IN_TN), lambda j, i: (i, j)),
        scratch_shapes=[pltpu.VMEM((D_MODEL, IN_TN), BF16)],
        compiler_params=_params("parallel", "arbitrary"),
        name="in_proj",
    )(h, w_in)


def _gmlp_kernel(*refs, tr, chunk, sample):
    (u_ref, sv_ref, k_ref, v_ref, ki_ref, sg_ref, sb_ref, ws_ref, bias_ref, kg_ref, kb_ref,
     oa_ref, k4_ref, v4_ref, kbf_ref, kiln_ref, kibf_ref, vx_ref) = refs[:18]
    u = jax.nn.gelu(u_ref[...])
    svn = _layer_norm_rows(jax.nn.gelu(sv_ref[...]), sg_ref[...], sb_ref[...])
    if sample:
        refs[18][...] = svn
    svb = svn.astype(BF16)
    row = lax.broadcasted_iota(jnp.int32, (tr, tr), 0)
    col = lax.broadcasted_iota(jnp.int32, (tr, tr), 1)
    shift = chunk.bit_length() - 1
    visible = ((row >> shift) == (col >> shift)) & (col <= row)
    for g in range(GMLP_GROUPS):
        cols = slice(g * GMLP_GDIM, (g + 1) * GMLP_GDIM)
        wm = jnp.where(visible, ws_ref[g], 0.0).astype(BF16)
        f = jnp.dot(wm, svb[:, cols], preferred_element_type=F32) + bias_ref[:, cols]
        oa_ref[:, cols] = (u[:, cols] * f).astype(BF16)
    k = k_ref[...]
    v = v_ref[...]
    for h in range(N_KV_HEADS):
        k4_ref[:, h, :] = k[:, h * HEAD_DIM:(h + 1) * HEAD_DIM]
        v4_ref[:, h, :] = v[:, h * HEAD_DIM:(h + 1) * HEAD_DIM]
    kbf_ref[...] = k.astype(BF16)
    vx_ref[...] = v.astype(BF16) if sample else v.T.astype(BF16)
    kiln = _layer_norm_rows(ki_ref[...], kg_ref[...], kb_ref[...])
    kiln_ref[...] = kiln
    kibf_ref[...] = kiln.astype(BF16)


def _gmlp(p, sgu_g, sgu_b, w_tiled, bias_tiled, ki_g, ki_b, chunk, tr, sample):
    m = p.shape[0]
    row = lambda c: (lambda i: (i, c))
    const2 = lambda i: (0, 0)
    out_shape = [
        jax.ShapeDtypeStruct((m, D_A), BF16),
        jax.ShapeDtypeStruct((m, N_KV_HEADS, HEAD_DIM), F32),
        jax.ShapeDtypeStruct((m, N_KV_HEADS, HEAD_DIM), F32),
        jax.ShapeDtypeStruct((m, KV_WIDTH), BF16),
        jax.ShapeDtypeStruct((m, IDX_DIM), F32),
        jax.ShapeDtypeStruct((m, IDX_DIM), BF16),
        jax.ShapeDtypeStruct((m, KV_WIDTH) if sample else (KV_WIDTH, m), BF16),
    ]
    head4 = pl.BlockSpec((tr, N_KV_HEADS, HEAD_DIM), lambda i: (i, 0, 0))
    out_specs = [
        pl.BlockSpec((tr, D_A), row(0)), head4, head4,
        pl.BlockSpec((tr, KV_WIDTH), row(0)),
        pl.BlockSpec((tr, IDX_DIM), row(0)),
        pl.BlockSpec((tr, IDX_DIM), row(0)),
        pl.BlockSpec((tr, KV_WIDTH), row(0)) if sample else pl.BlockSpec((KV_WIDTH, tr), lambda i: (0, i)),
    ]
    if sample:
        out_shape.append(jax.ShapeDtypeStruct((m, D_A), F32))
        out_specs.append(pl.BlockSpec((tr, D_A), row(0)))
    return pl.pallas_call(
        functools.partial(_gmlp_kernel, tr=tr, chunk=chunk, sample=sample),
        out_shape=out_shape,
        grid=(m // tr,),
        in_specs=[
            pl.BlockSpec((tr, D_A), row(COL_U // D_A)),
            pl.BlockSpec((tr, D_A), row(COL_SV // D_A)),
            pl.BlockSpec((tr, KV_WIDTH), row(COL_K // KV_WIDTH)),
            pl.BlockSpec((tr, KV_WIDTH), row(COL_V // KV_WIDTH)),
            pl.BlockSpec((tr, IDX_DIM), row(COL_KI // IDX_DIM)),
            pl.BlockSpec((1, D_A), const2),
            pl.BlockSpec((1, D_A), const2),
            pl.BlockSpec((GMLP_GROUPS, tr, tr), lambda i: (0, 0, 0)),
            pl.BlockSpec((tr, D_A), const2),
            pl.BlockSpec((1, IDX_DIM), const2),
            pl.BlockSpec((1, IDX_DIM), const2),
        ],
        out_specs=out_specs,
        compiler_params=_params("parallel"),
        name="gmlp",
    )(p, p, p, p, p, sgu_g, sgu_b, w_tiled, bias_tiled, ki_g, ki_b)


def _dsa_prompt_body(i, q_ref, qi0_ref, qi1_ref, wi_ref, k_ref, vt_ref, ki_ref, o_ref, score_ref, *, tq, n_keys, n_sel):
    ki = ki_ref[0:n_keys, :]
    wit = wi_ref[...].T
    half = IDX_HEADS // 2
    for pair in range(half):
        qi_ref_p = qi0_ref if pair < half // 2 else qi1_ref
        c0 = (2 * pair) % half * IDX_DIM
        rhs = jnp.concatenate([qi_ref_p[:, c0:c0 + IDX_DIM], qi_ref_p[:, c0 + IDX_DIM:c0 + 2 * IDX_DIM]],
                              axis=0).astype(BF16)
        logits = lax.dot_general(ki, rhs, NT_DIMS, preferred_element_type=F32)
        part = (jnp.maximum(logits[:, :tq], 0.0) * wit[2 * pair:2 * pair + 1, :]
                + jnp.maximum(logits[:, tq:], 0.0) * wit[2 * pair + 1:2 * pair + 2, :])
        if pair == 0:
            score_ref[0:n_keys, :] = part
        else:
            score_ref[0:n_keys, :] += part
    kpos = lax.broadcasted_iota(jnp.int32, (n_keys, tq), 0)
    qpos = i * tq + lax.broadcasted_iota(jnp.int32, (n_keys, tq), 1)
    admissible = (kpos >> CHUNK_SHIFT) <= (qpos >> CHUNK_SHIFT)
    score = jnp.where(admissible, score_ref[0:n_keys, :] * IDX_SCALE, -jnp.inf)
    score_ref[0:n_keys, :] = score
    n_adm = jnp.sum(jnp.where(admissible, 1.0, 0.0), axis=0, keepdims=True)

    def count_ge(key):
        return jnp.sum(jnp.where(score_ref[0:n_keys, :] >= _key_to_float(key), 1.0, 0.0), axis=0, keepdims=True)

    prefix = _kth_largest_key(count_ge, (1, tq), n_sel)
    tau = jnp.where(n_adm <= n_sel, -jnp.inf, _key_to_float(prefix))
    bias = jnp.where(admissible, jnp.where(score_ref[0:n_keys, :] >= tau, 0.0, -jnp.inf), -jnp.inf)

    for hk in range(N_KV_HEADS):
        heads = [hk * Q_PER_KV + g for g in range(Q_PER_KV)]
        qs = jnp.concatenate([q_ref[:, h * HEAD_DIM:(h + 1) * HEAD_DIM] for h in heads], axis=0).astype(BF16)
        kh = k_ref[0:n_keys, hk * HEAD_DIM:(hk + 1) * HEAD_DIM]
        s = lax.dot_general(kh, qs, NT_DIMS, preferred_element_type=F32) * ATTN_SCALE
        ps, ls = [], []
        for g in range(Q_PER_KV):
            sg = s[:, g * tq:(g + 1) * tq] + bias
            pg = jnp.exp(sg - jnp.max(sg, axis=0, keepdims=True))
            ls.append(jnp.sum(pg, axis=0, keepdims=True))
            ps.append(pg.astype(BF16))
        ot = jnp.dot(vt_ref[hk * HEAD_DIM:(hk + 1) * HEAD_DIM, 0:n_keys], jnp.concatenate(ps, axis=1),
                     preferred_element_type=F32)
        ot = ot / jnp.concatenate(ls, axis=1)
        for g, h in enumerate(heads):
            o_ref[:, h * HEAD_DIM:(h + 1) * HEAD_DIM] = ot[:, g * tq:(g + 1) * tq].T.astype(BF16)


def _dsa_prompt_kernel(*refs, tq, nq, n_sel, n_buckets):
    i = pl.program_id(1)
    per = nq // n_buckets
    for c in range(n_buckets):
        @pl.when((i >= c * per) & (i < (c + 1) * per))
        def _(c=c):
            _dsa_prompt_body(i, *refs, tq=tq, n_keys=(c + 1) * per * tq, n_sel=n_sel)


def _dsa_prompt(p, kbf, vt, kibf, nb, t):
    tq = 128
    nq = t // tq
    n_sel = min(TOPK_MAX, t // 4)
    qrow = lambda c: (lambda b, i: (b * nq + i, c))
    return pl.pallas_call(
        functools.partial(_dsa_prompt_kernel, tq=tq, nq=nq, n_sel=n_sel, n_buckets=4),
        out_shape=jax.ShapeDtypeStruct((nb * t, D_B), BF16),
        grid=(nb, nq),
        in_specs=[
            pl.BlockSpec((tq, D_B), qrow(COL_Q // D_B)),
            pl.BlockSpec((tq, IDX_WIDTH // 2), qrow(2 * COL_QI // IDX_WIDTH)),
            pl.BlockSpec((tq, IDX_WIDTH // 2), qrow(2 * COL_QI // IDX_WIDTH + 1)),
            pl.BlockSpec((tq, LANE), qrow(COL_WI // LANE)),
            pl.BlockSpec((t, KV_WIDTH), lambda b, i: (b, 0)),
            pl.BlockSpec((KV_WIDTH, t), lambda b, i: (0, b)),
            pl.BlockSpec((t, IDX_DIM), lambda b, i: (b, 0)),
        ],
        out_specs=pl.BlockSpec((tq, D_B), lambda b, i: (b * nq + i, 0)),
        scratch_shapes=[pltpu.VMEM((t, tq), F32)],
        compiler_params=_params("parallel", "parallel"),
        name="dsa_prompt",
    )(p, p, p, p, kbf, vt, kibf)


def _dsa_sample_kernel(q_ref, qi0_ref, qi1_ref, wi_ref, kn_ref, vn_ref, kin_ref, kp_ref, vp_ref, kip_ref,
                       o_ref, kp_s, vp_s, kip_s, sp_ref, *, tq, n_chunks, chunk_len, n_sel):
    c = pl.program_id(1)
    off = pl.multiple_of(c * chunk_len, chunk_len)
    for h in range(N_KV_HEADS):
        kp_s[h, pl.ds(off, chunk_len), :] = kp_ref[0, 0, :, h, :].astype(BF16)
        vp_s[h, pl.ds(off, chunk_len), :] = vp_ref[0, 0, :, h, :].astype(BF16)
    kip_s[pl.ds(off, chunk_len), :] = kip_ref[0, 0].astype(BF16)

    @pl.when(c == n_chunks - 1)
    def _():
        half = IDX_HEADS // 2
        wi = wi_ref[...]
        qi_rows = jnp.concatenate(
            [(qi0_ref if h < half else qi1_ref)[:, (h % half) * IDX_DIM:(h % half + 1) * IDX_DIM]
             for h in range(IDX_HEADS)], axis=0).astype(BF16)

        def index_scores(keys):
            logits = lax.dot_general(qi_rows, keys, NT_DIMS, preferred_element_type=F32)
            total = jnp.maximum(logits[0:tq], 0.0) * wi[:, 0:1]
            for h in range(1, IDX_HEADS):
                total = total + jnp.maximum(logits[h * tq:(h + 1) * tq], 0.0) * wi[:, h:h + 1]
            return total * IDX_SCALE

        for j in range(n_chunks):
            sp_ref[:, j * chunk_len:(j + 1) * chunk_len] = index_scores(kip_s[j * chunk_len:(j + 1) * chunk_len, :])
        s_new = index_scores(kin_ref[...])

        def count_ge(key):
            cf = _key_to_float(key)
            return (jnp.sum(jnp.where(sp_ref[...] >= cf, 1.0, 0.0), axis=-1, keepdims=True)
                    + jnp.sum(jnp.where(s_new >= cf, 1.0, 0.0), axis=-1, keepdims=True))

        tau = _key_to_float(_kth_largest_key(count_ge, (tq, 1), n_sel))
        bias_p = jnp.where(sp_ref[...] >= tau, 0.0, -jnp.inf)
        bias_n = jnp.where(s_new >= tau, 0.0, -jnp.inf)

        for hk in range(N_KV_HEADS):
            heads = [hk * Q_PER_KV + g for g in range(Q_PER_KV)]
            cols = slice(hk * HEAD_DIM, (hk + 1) * HEAD_DIM)
            qs = jnp.concatenate([q_ref[:, h * HEAD_DIM:(h + 1) * HEAD_DIM] for h in heads], axis=0).astype(BF16)
            s_p = lax.dot_general(qs, kp_s[hk], NT_DIMS, preferred_element_type=F32) * ATTN_SCALE
            s_n = lax.dot_general(qs, kn_ref[:, cols], NT_DIMS, preferred_element_type=F32) * ATTN_SCALE
            pps, pns, ls = [], [], []
            for g in range(Q_PER_KV):
                rows = slice(g * tq, (g + 1) * tq)
                sp_g = s_p[rows] + bias_p
                sn_g = s_n[rows] + bias_n
                mx = jnp.maximum(jnp.max(sp_g, axis=-1, keepdims=True), jnp.max(sn_g, axis=-1, keepdims=True))
                pp = jnp.exp(sp_g - mx)
                pn = jnp.exp(sn_g - mx)
                ls.append(jnp.sum(pp, axis=-1, keepdims=True) + jnp.sum(pn, axis=-1, keepdims=True))
                pps.append(pp.astype(BF16))
                pns.append(pn.astype(BF16))
            o = (jnp.dot(jnp.concatenate(pps, axis=0), vp_s[hk], preferred_element_type=F32)
                 + jnp.dot(jnp.concatenate(pns, axis=0), vn_ref[:, cols], preferred_element_type=F32))
            for g, h in enumerate(heads):
                o_ref[:, h * HEAD_DIM:(h + 1) * HEAD_DIM] = (o[g * tq:(g + 1) * tq] / ls[g]).astype(BF16)


def _dsa_sample(p, kbf, vbf, kibf, cache_k, cache_v, cache_kidx, l, nb, t):
    assert PAST_LEN % (1 << CHUNK_SHIFT) == 0 and t <= (1 << CHUNK_SHIFT)
    n_chunks = 4
    chunk_len = PAST_LEN // n_chunks
    n_sel = min(TOPK_MAX, (PAST_LEN + t) // 4)
    qrow = lambda c: (lambda b, j: (b, c))
    new = lambda b, j: (b, 0)
    past5 = pl.BlockSpec((1, 1, chunk_len, N_KV_HEADS, HEAD_DIM), lambda b, j: (l, b, j, 0, 0))
    return pl.pallas_call(
        functools.partial(_dsa_sample_kernel, tq=t, n_chunks=n_chunks, chunk_len=chunk_len, n_sel=n_sel),
        out_shape=jax.ShapeDtypeStruct((nb * t, D_B), BF16),
        grid=(nb, n_chunks),
        in_specs=[
            pl.BlockSpec((t, D_B), qrow(COL_Q // D_B)),
            pl.BlockSpec((t, IDX_WIDTH // 2), qrow(2 * COL_QI // IDX_WIDTH)),
            pl.BlockSpec((t, IDX_WIDTH // 2), qrow(2 * COL_QI // IDX_WIDTH + 1)),
            pl.BlockSpec((t, LANE), qrow(COL_WI // LANE)),
            pl.BlockSpec((t, KV_WIDTH), new),
            pl.BlockSpec((t, KV_WIDTH), new),
            pl.BlockSpec((t, IDX_DIM), new),
            past5, past5,
            pl.BlockSpec((1, 1, chunk_len, IDX_DIM), lambda b, j: (l, b, j, 0)),
        ],
        out_specs=pl.BlockSpec((t, D_B), new),
        scratch_shapes=[pltpu.VMEM((N_KV_HEADS, PAST_LEN, HEAD_DIM), BF16),
                        pltpu.VMEM((N_KV_HEADS, PAST_LEN, HEAD_DIM), BF16),
                        pltpu.VMEM((PAST_LEN, IDX_DIM), BF16),
                        pltpu.VMEM((t, PAST_LEN), F32)],
        compiler_params=_params("parallel", "arbitrary"),
        name="dsa_sample",
    )(p, p, p, p, kbf, vbf, kibf, cache_k, cache_v, cache_kidx)


def _merge_kernel(h_ref, a_ref, b_ref, wga_ref, wgb_ref, wa_ref, wb_ref, o_ref):
    h = h_ref[...]
    ga = jax.nn.sigmoid(jnp.dot(h, wga_ref[0], preferred_element_type=F32))
    gb = jax.nn.sigmoid(jnp.dot(h, wgb_ref[0], preferred_element_type=F32))
    ya = jnp.dot(a_ref[...], wa_ref[0], preferred_element_type=F32)
    yb = jnp.dot(b_ref[...], wb_ref[0], preferred_element_type=F32)
    o_ref[...] = (ga * ya + gb * yb).astype(BF16)


def _merge(h, out_a, out_b, w_gates, w_pa, w_pb, l, tm):
    m = h.shape[0]
    tn = 512
    nn = D_MODEL // tn
    rows = pl.BlockSpec((tm, D_MODEL), lambda j, i: (i, 0))
    wspec = lambda off: pl.BlockSpec((1, D_MODEL, tn), lambda j, i: (l, 0, j + off))
    return pl.pallas_call(
        _merge_kernel,
        out_shape=jax.ShapeDtypeStruct((m, D_MODEL), BF16),
        grid=(nn, m // tm),
        in_specs=[rows, rows, rows, wspec(0), wspec(nn), wspec(0), wspec(0)],
        out_specs=pl.BlockSpec((tm, tn), lambda j, i: (i, j)),
        compiler_params=_params("parallel", "parallel"),
        name="merge",
    )(h, out_a, out_b, w_gates, w_gates, w_pa, w_pb)


def _run_trunk(x, mods, wts, nb, t, tm, tr, chunk, caches):
    sample = caches is not None
    tpg = max(t // tm, 1)
    h = _modulate(x, mods(0, 0, 0), mods(0, 0, 1), tm, tpg)
    new_k, new_v, new_ki, new_sv = [], [], [], []
    for l in range(DEPTH):
        ln = lambda s: (wts["ln_g"][l, s][None], wts["ln_b"][l, s][None])
        act = _ffn_up(h, wts["up"], l, 0, tm)
        x, h = _proj_resid_ln(act, wts["down"], l, 0, x, mods(l, 0, 2), *ln(0),
                              (mods(l, 1, 0), mods(l, 1, 1)), 0.5, tm, tpg)
        p = _in_proj(h, wts["in"], l, tm)
        outs = _gmlp(p, wts["sgu_g"][l][None], wts["sgu_b"][l][None], *wts["spatial"][l][chunk],
                     wts["ki_g"][l][None], wts["ki_b"][l][None], chunk, tr, sample)
        out_a, k4, v4, kbf, kiln, kibf, vx = outs[:7]
        if sample:
            out_b = _dsa_sample(p, kbf, vx, kibf, *caches, l, nb, t)
            new_sv.append(outs[7].reshape(nb, t, D_A))
        else:
            out_b = _dsa_prompt(p, kbf, vx, kibf, nb, t)
        z = _merge(h, out_a, out_b, wts["gates"], wts["pa"], wts["pb"], l, tm)
        x, h = _proj_resid_ln(z, wts["out"], l, 0, x, mods(l, 1, 2), *ln(1),
                              (mods(l, 2, 0), mods(l, 2, 1)), 1.0, tm, tpg)
        act = _ffn_up(h, wts["up"], l, 1, tm)
        nxt = (mods(l + 1, 0, 0), mods(l + 1, 0, 1)) if l + 1 < DEPTH else None
        x, h = _proj_resid_ln(act, wts["down"], l, 1, x, mods(l, 2, 2), *ln(2), nxt, 0.5, tm, tpg)
        new_k.append(k4.reshape(nb, t, N_KV_HEADS, HEAD_DIM))
        new_v.append(v4.reshape(nb, t, N_KV_HEADS, HEAD_DIM))
        new_ki.append(kiln.reshape(nb, t, IDX_DIM))
    return x, jnp.stack(new_k), jnp.stack(new_v), jnp.stack(new_ki), (jnp.stack(new_sv) if sample else None)


def kernel(x_prompt, x_sample, c_prompt, c_sample, cache_k, cache_v, cache_kidx, w_ada, b_ada, w_ffn_up,
           w_ffn_down, w_in, sgu_ln_g, sgu_ln_b, w_spatial, b_spatial, kidx_ln_g, kidx_ln_b, w_branch_a,
           w_branch_b, w_out, ln_g, ln_b):
    tr = 256
    c_all = jnp.concatenate([c_prompt, c_sample, jnp.zeros((ADA_ROWS - BATCH - DEC_BATCH, D_MODEL), F32)], axis=0)
    ada = _ada(c_all, w_ada, b_ada)

    def mod_cols(l, sub, kind, rows):
        c0 = (sub * 3 + kind) * D_MODEL
        return ada[l, rows, c0:c0 + D_MODEL]

    def mods_prompt(l, sub, kind):
        return mod_cols(l, sub, kind, slice(0, BATCH))[:, None, :]

    def mods_sample(l, sub, kind):
        rows = mod_cols(l, sub, kind, slice(BATCH, BATCH + DEC_BATCH))
        return jnp.repeat(rows, DEC_SEQ, axis=0)[None]

    def spatial(l, chunk):
        reps = tr // chunk
        w = jnp.tile(w_spatial[l][:, :chunk, :chunk], (1, reps, reps))
        b = jnp.tile(jnp.repeat(jnp.transpose(b_spatial[l][:, :chunk]), GMLP_GDIM, axis=1), (reps, 1))
        return w, b

    wts = dict(
        up=w_ffn_up, **{"in": w_in},
        down=w_ffn_down.astype(BF16),
        out=w_out.astype(BF16)[:, None],
        pa=w_branch_a.astype(BF16), pb=w_branch_b.astype(BF16),
        gates=w_in[:, :, COL_GATES:COL_GATES + 2 * D_MODEL].astype(BF16),
        ln_g=ln_g, ln_b=ln_b, sgu_g=sgu_ln_g, sgu_b=sgu_ln_b, ki_g=kidx_ln_g, ki_b=kidx_ln_b,
        spatial=[{c: spatial(l, c) for c in (GMLP_CHUNK, DEC_SEQ)} for l in range(DEPTH)],
    )

    y_p, k_p, v_p, ki_p, _ = _run_trunk(x_prompt.reshape(BATCH * SEQ, D_MODEL), mods_prompt, wts,
                                        BATCH, SEQ, 512, tr, GMLP_CHUNK, None)
    m_s = DEC_BATCH * DEC_SEQ
    y_s, k_s, v_s, ki_s, sv_s = _run_trunk(x_sample.reshape(m_s, D_MODEL), mods_sample, wts,
                                           DEC_BATCH, DEC_SEQ, m_s, tr, DEC_SEQ, (cache_k, cache_v, cache_kidx))
    return (y_p.reshape(BATCH, SEQ, D_MODEL), y_s.reshape(DEC_BATCH, DEC_SEQ, D_MODEL),
            k_p, v_p, ki_p, k_s, v_s, ki_s, sv_s)
```

```python
import functools

import jax
import jax.numpy as jnp
from jax import lax
from jax.experimental import pallas as pl
from jax.experimental.pallas import tpu as pltpu

F32 = jnp.float32
BF16 = jnp.bfloat16

D_MODEL = 2048
BATCH = 4
SEQ = 2048
DEPTH = 4
DEC_BATCH = 8
DEC_SEQ = 32
PAST_LEN = 4096
CHUNK_SHIFT = 6
GMLP_CHUNK = 128
GMLP_GROUPS = 8
D_A = D_MODEL
GMLP_GDIM = D_A // GMLP_GROUPS
HEAD_DIM = 128
N_HEADS = D_MODEL // HEAD_DIM
N_KV_HEADS = 4
Q_PER_KV = N_HEADS // N_KV_HEADS
D_B = N_HEADS * HEAD_DIM
KV_WIDTH = N_KV_HEADS * HEAD_DIM
IDX_HEADS = 16
IDX_DIM = 128
IDX_WIDTH = IDX_HEADS * IDX_DIM
TOPK_MAX = 256
D_FF = 5632
ALPHA = (2 * DEPTH) ** 0.25
LN_EPS = 1e-5
IDX_SCALE = (IDX_HEADS * IDX_DIM) ** -0.5
ATTN_SCALE = HEAD_DIM ** -0.5
LOG2_E = 1.4426950408889634
IDX_KEY_CHUNK = 512
ADA_WIDTH = 9 * D_MODEL
ADA_ROWS = 16

COL_U = 0
COL_SV = D_A
COL_Q = 2 * D_A
COL_K = COL_Q + D_B
COL_V = COL_K + KV_WIDTH
COL_QI = COL_V + KV_WIDTH
COL_KI = COL_QI + IDX_WIDTH
COL_WI = COL_KI + IDX_DIM
COL_GATES = COL_WI + IDX_HEADS
IN_TN = 1024
IN_MAIN_WIDTH = 10 * IN_TN

LANE = 128
VMEM_LIMIT = 56 * 1024 * 1024
NT_DIMS = (((1,), (1,)), ((), ()))


def _params(*sem):
    return pltpu.CompilerParams(dimension_semantics=sem, vmem_limit_bytes=VMEM_LIMIT)


def _layer_norm_rows(r, g, b):
    mu = jnp.mean(r, axis=-1, keepdims=True)
    rc = r - mu
    var = jnp.mean(rc * rc, axis=-1, keepdims=True)
    return rc * lax.rsqrt(var + LN_EPS) * g + b


def _key_to_float(key):
    return lax.bitcast_convert_type(key ^ ((key >> 31) & 0x7FFFFFFF), F32)


def _kth_largest_key(count_ge, shape, n_sel):
    zero = jnp.zeros(shape, jnp.int32)
    prefix = jnp.where(count_ge(zero) >= n_sel, zero, jnp.int32(-2 ** 31))
    for bit in range(30, -1, -1):
        cand = prefix | jnp.int32(1 << bit)
        prefix = jnp.where(count_ge(cand) >= n_sel, cand, prefix)
    return prefix


def _ada_kernel(c_ref, w_ref, b_ref, o_ref):
    c = c_ref[...]
    s = (c * jax.nn.sigmoid(c)).astype(BF16)
    o_ref[0] = jnp.dot(s, w_ref[0].astype(BF16), preferred_element_type=F32) + b_ref[0]


def _ada(c_all, w_ada, b_ada):
    tn = 1024
    return pl.pallas_call(
        _ada_kernel,
        out_shape=jax.ShapeDtypeStruct((DEPTH, ADA_ROWS, ADA_WIDTH), F32),
        grid=(DEPTH, ADA_WIDTH // tn),
        in_specs=[
            pl.BlockSpec((ADA_ROWS, D_MODEL), lambda l, j: (0, 0)),
            pl.BlockSpec((1, D_MODEL, tn), lambda l, j: (l, 0, j)),
            pl.BlockSpec((1, 1, tn), lambda l, j: (l, 0, j)),
        ],
        out_specs=pl.BlockSpec((1, ADA_ROWS, tn), lambda l, j: (l, 0, j)),
        compiler_params=_params("parallel", "parallel"),
        name="ada",
    )(c_all, w_ada, b_ada.reshape(DEPTH, 1, ADA_WIDTH))


def _mod_kernel(x_ref, sh_ref, sc_ref, h_ref):
    h_ref[...] = (x_ref[...] * (1.0 + sc_ref[0]) + sh_ref[0]).astype(BF16)


def _mod_spec(mod, tiles_per_group):
    return pl.BlockSpec((1, mod.shape[1], D_MODEL), lambda i, *_: (i // tiles_per_group, 0, 0))


def _modulate(x, shift, scale, tm, tpg):
    m = x.shape[0]
    return pl.pallas_call(
        _mod_kernel,
        out_shape=jax.ShapeDtypeStruct((m, D_MODEL), BF16),
        grid=(m // tm,),
        in_specs=[pl.BlockSpec((tm, D_MODEL), lambda i: (i, 0)), _mod_spec(shift, tpg), _mod_spec(scale, tpg)],
        out_specs=pl.BlockSpec((tm, D_MODEL), lambda i: (i, 0)),
        compiler_params=_params("parallel"),
        name="modulate",
    )(x, shift, scale)


def _up_kernel(h_ref, wa_ref, wb_ref, o_ref, wa_s, wb_s):
    @pl.when(pl.program_id(1) == 0)
    def _():
        wa_s[...] = wa_ref[0, 0].astype(BF16)
        wb_s[...] = wb_ref[0, 0].astype(BF16)

    h = h_ref[...]
    a = jnp.dot(h, wa_s[...], preferred_element_type=F32)
    b = jnp.dot(h, wb_s[...], preferred_element_type=F32)
    o_ref[...] = (a * jax.nn.sigmoid(a) * b).astype(BF16)


def _ffn_up(h, w_up, l, s, tm):
    m = h.shape[0]
    tf = 512
    nf = D_FF // tf
    return pl.pallas_call(
        _up_kernel,
        out_shape=jax.ShapeDtypeStruct((m, D_FF), BF16),
        grid=(nf, m // tm),
        in_specs=[
            pl.BlockSpec((tm, D_MODEL), lambda j, i: (i, 0)),
            pl.BlockSpec((1, 1, D_MODEL, tf), lambda j, i: (l, s, 0, j)),
            pl.BlockSpec((1, 1, D_MODEL, tf), lambda j, i: (l, s, 0, j + nf)),
        ],
        out_specs=pl.BlockSpec((tm, tf), lambda j, i: (i, j)),
        scratch_shapes=[pltpu.VMEM((D_MODEL, tf), BF16), pltpu.VMEM((D_MODEL, tf), BF16)],
        compiler_params=_params("parallel", "arbitrary"),
        name="ffn_up",
    )(h, w_up, w_up)


def _down_kernel(*refs, gscale, nk, emit_h):
    if emit_h:
        a_ref, w_ref, x_ref, g_ref, lg_ref, lb_ref, sh_ref, sc_ref, xo_ref, ho_ref, acc_ref = refs
    else:
        a_ref, w_ref, x_ref, g_ref, lg_ref, lb_ref, xo_ref, acc_ref = refs
    k = pl.program_id(1)
    part = jnp.dot(a_ref[...], w_ref[0, 0], preferred_element_type=F32)

    def finish(y):
        r = ALPHA * x_ref[...] + (gscale * g_ref[0]) * y
        xn = _layer_norm_rows(r, lg_ref[...], lb_ref[...])
        xo_ref[...] = xn
        if emit_h:
            ho_ref[...] = (xn * (1.0 + sc_ref[0]) + sh_ref[0]).astype(BF16)

    if nk == 1:
        finish(part)
        return

    @pl.when(k == 0)
    def _():
        acc_ref[...] = part

    @pl.when((k > 0) & (k < nk - 1))
    def _():
        acc_ref[...] += part

    @pl.when(k == nk - 1)
    def _():
        finish(acc_ref[...] + part)


def _proj_resid_ln(a, w, l, s, x, gate, lg, lb, nxt, gscale, tm, tpg):
    m, kdim = a.shape
    tk = 1408 if kdim == D_FF else kdim
    nk = kdim // tk
    emit_h = nxt is not None
    row = lambda i, k: (i, 0)
    const = lambda i, k: (0, 0)
    in_specs = [
        pl.BlockSpec((tm, tk), lambda i, k: (i, k)),
        pl.BlockSpec((1, 1, tk, D_MODEL), lambda i, k: (l, s, k, 0)),
        pl.BlockSpec((tm, D_MODEL), row),
        _mod_spec(gate, tpg),
        pl.BlockSpec((1, D_MODEL), const),
        pl.BlockSpec((1, D_MODEL), const),
    ]
    args = [a, w, x, gate, lg, lb]
    out_shape = [jax.ShapeDtypeStruct((m, D_MODEL), F32)]
    out_specs = [pl.BlockSpec((tm, D_MODEL), row)]
    if emit_h:
        in_specs += [_mod_spec(nxt[0], tpg), _mod_spec(nxt[1], tpg)]
        args += [nxt[0], nxt[1]]
        out_shape.append(jax.ShapeDtypeStruct((m, D_MODEL), BF16))
        out_specs.append(pl.BlockSpec((tm, D_MODEL), row))
    outs = pl.pallas_call(
        functools.partial(_down_kernel, gscale=gscale, nk=nk, emit_h=emit_h),
        out_shape=out_shape,
        grid=(m // tm, nk),
        in_specs=in_specs,
        out_specs=out_specs,
        scratch_shapes=[pltpu.VMEM((tm, D_MODEL) if nk > 1 else (8, LANE), F32)],
        compiler_params=_params("parallel", "arbitrary"),
        name="proj_resid_ln",
    )(*args)
    return (outs[0], outs[1]) if emit_h else (outs[0], None)


def _in_proj_kernel(h_ref, w_ref, o_ref, w_s):
    @pl.when(pl.program_id(1) == 0)
    def _():
        w_s[...] = w_ref[0].T.astype(BF16)

    o_ref[...] = jnp.dot(h_ref[...], w_s[...], preferred_element_type=F32)


def _in_proj(h, w_in_t, l, tm):
    m = h.shape[0]
    return pl.pallas_call(
        _in_proj_kernel,
        out_shape=jax.ShapeDtypeStruct((m, IN_MAIN_WIDTH), F32),
        grid=(IN_MAIN_WIDTH // IN_TN, m // tm),
        in_specs=[pl.BlockSpec((tm, D_MODEL), lambda j, i: (i, 0)),
                  pl.BlockSpec((1, IN_TN, D_MODEL), lambda j, i: (l, j, 0))],
        out_specs=pl.BlockSpec((tm, IN_TN), lambda j, i: (i, j)),
        scratch_shapes=[pltpu.VMEM((D_MODEL, IN_TN), BF16)],
        compiler_params=_params("parallel", "arbitrary"),
        name="in_proj",
    )(h, w_in_t)


def _gmlp_kernel(*refs, tr, chunk, sample):
    (u_ref, sv_ref, k_ref, v_ref, ki_ref, sg_ref, sb_ref, ws_ref, bias_ref, kg_ref, kb_ref,
     oa_ref, k4_ref, v4_ref, kbf_ref, kiln_ref, kibf_ref, vx_ref) = refs[:18]
    u = jax.nn.gelu(u_ref[...])
    svn = _layer_norm_rows(jax.nn.gelu(sv_ref[...]), sg_ref[...], sb_ref[...])
    if sample:
        refs[18][...] = svn
    svb = svn.astype(BF16)
    row = lax.broadcasted_iota(jnp.int32, (tr, tr), 0)
    col = lax.broadcasted_iota(jnp.int32, (tr, tr), 1)
    shift = chunk.bit_length() - 1
    visible = ((row >> shift) == (col >> shift)) & (col <= row)
    for g in range(GMLP_GROUPS):
        cols = slice(g * GMLP_GDIM, (g + 1) * GMLP_GDIM)
        wm = jnp.where(visible, ws_ref[g], 0.0).astype(BF16)
        f = jnp.dot(wm, svb[:, cols], preferred_element_type=F32) + bias_ref[:, cols]
        oa_ref[:, cols] = (u[:, cols] * f).astype(BF16)
    k = k_ref[...]
    v = v_ref[...]
    for h in range(N_KV_HEADS):
        k4_ref[:, h, :] = k[:, h * HEAD_DIM:(h + 1) * HEAD_DIM]
        v4_ref[:, h, :] = v[:, h * HEAD_DIM:(h + 1) * HEAD_DIM]
    kbf_ref[...] = k.astype(BF16)
    vx_ref[...] = v.astype(BF16)
    kiln = _layer_norm_rows(ki_ref[...], kg_ref[...], kb_ref[...])
    kiln_ref[...] = kiln
    kibf_ref[...] = kiln.astype(BF16)


def _gmlp(p, sgu_g, sgu_b, w_tiled, bias_tiled, ki_g, ki_b, chunk, tr, sample):
    m = p.shape[0]
    row = lambda c: (lambda i: (i, c))
    const2 = lambda i: (0, 0)
    out_shape = [
        jax.ShapeDtypeStruct((m, D_A), BF16),
        jax.ShapeDtypeStruct((m, N_KV_HEADS, HEAD_DIM), F32),
        jax.ShapeDtypeStruct((m, N_KV_HEADS, HEAD_DIM), F32),
        jax.ShapeDtypeStruct((m, KV_WIDTH), BF16),
        jax.ShapeDtypeStruct((m, IDX_DIM), F32),
        jax.ShapeDtypeStruct((m, IDX_DIM), BF16),
        jax.ShapeDtypeStruct((m, KV_WIDTH), BF16),
    ]
    head4 = pl.BlockSpec((tr, N_KV_HEADS, HEAD_DIM), lambda i: (i, 0, 0))
    out_specs = [
        pl.BlockSpec((tr, D_A), row(0)), head4, head4,
        pl.BlockSpec((tr, KV_WIDTH), row(0)),
        pl.BlockSpec((tr, IDX_DIM), row(0)),
        pl.BlockSpec((tr, IDX_DIM), row(0)),
        pl.BlockSpec((tr, KV_WIDTH), row(0)),
    ]
    if sample:
        out_shape.append(jax.ShapeDtypeStruct((m, D_A), F32))
        out_specs.append(pl.BlockSpec((tr, D_A), row(0)))
    return pl.pallas_call(
        functools.partial(_gmlp_kernel, tr=tr, chunk=chunk, sample=sample),
        out_shape=out_shape,
        grid=(m // tr,),
        in_specs=[
            pl.BlockSpec((tr, D_A), row(COL_U // D_A)),
            pl.BlockSpec((tr, D_A), row(COL_SV // D_A)),
            pl.BlockSpec((tr, KV_WIDTH), row(COL_K // KV_WIDTH)),
            pl.BlockSpec((tr, KV_WIDTH), row(COL_V // KV_WIDTH)),
            pl.BlockSpec((tr, IDX_DIM), row(COL_KI // IDX_DIM)),
            pl.BlockSpec((1, D_A), const2),
            pl.BlockSpec((1, D_A), const2),
            pl.BlockSpec((GMLP_GROUPS, tr, tr), lambda i: (0, 0, 0)),
            pl.BlockSpec((tr, D_A), const2),
            pl.BlockSpec((1, IDX_DIM), const2),
            pl.BlockSpec((1, IDX_DIM), const2),
        ],
        out_specs=out_specs,
        compiler_params=_params("parallel"),
        name="gmlp",
    )(p, p, p, p, p, sgu_g, sgu_b, w_tiled, bias_tiled, ki_g, ki_b)


def _dsa_prompt_body(i, q_ref, qi0_ref, qi1_ref, wi_ref, k_ref, v_ref, ki_ref, o_ref, score_ref, *, tq, n_keys, n_sel):
    half = IDX_HEADS // 2
    wi = wi_ref[...]
    qi_rows = jnp.concatenate(
        [(qi0_ref if h < half else qi1_ref)[:, (h % half) * IDX_DIM:(h % half + 1) * IDX_DIM]
         for h in range(IDX_HEADS)], axis=0).astype(BF16)
    for c0 in range(0, n_keys, IDX_KEY_CHUNK):
        logits = lax.dot_general(qi_rows, ki_ref[c0:c0 + IDX_KEY_CHUNK, :], NT_DIMS, preferred_element_type=F32)
        total = jnp.maximum(logits[0:tq], 0.0) * wi[:, 0:1]
        for h in range(1, IDX_HEADS):
            total = total + jnp.maximum(logits[h * tq:(h + 1) * tq], 0.0) * wi[:, h:h + 1]
        score_ref[:, c0:c0 + IDX_KEY_CHUNK] = total * IDX_SCALE
    qpos = i * tq + lax.broadcasted_iota(jnp.int32, (tq, n_keys), 0)
    kpos = lax.broadcasted_iota(jnp.int32, (tq, n_keys), 1)
    admissible = (kpos >> CHUNK_SHIFT) <= (qpos >> CHUNK_SHIFT)
    score_ref[:, 0:n_keys] = jnp.where(admissible, score_ref[:, 0:n_keys], -jnp.inf)
    n_adm = jnp.sum(jnp.where(admissible, 1.0, 0.0), axis=-1, keepdims=True)

    def count_ge(key):
        return jnp.sum(jnp.where(score_ref[:, 0:n_keys] >= _key_to_float(key), 1.0, 0.0), axis=-1, keepdims=True)

    prefix = _kth_largest_key(count_ge, (tq, 1), n_sel)
    tau = jnp.where(n_adm <= n_sel, -jnp.inf, _key_to_float(prefix))
    bias = jnp.where(admissible, jnp.where(score_ref[:, 0:n_keys] >= tau, 0.0, -jnp.inf), -jnp.inf)

    for hk in range(N_KV_HEADS):
        heads = [hk * Q_PER_KV + g for g in range(Q_PER_KV)]
        cols = slice(hk * HEAD_DIM, (hk + 1) * HEAD_DIM)
        qs = jnp.concatenate([q_ref[:, h * HEAD_DIM:(h + 1) * HEAD_DIM] for h in heads], axis=0)
        qs = (qs * (ATTN_SCALE * LOG2_E)).astype(BF16)
        s = lax.dot_general(qs, k_ref[0:n_keys, cols], NT_DIMS, preferred_element_type=F32)
        ps, ls = [], []
        for g in range(Q_PER_KV):
            sg = s[g * tq:(g + 1) * tq] + bias
            pg = jnp.exp2(sg - jnp.max(sg, axis=-1, keepdims=True))
            ls.append(jnp.sum(pg, axis=-1, keepdims=True))
            ps.append(pg.astype(BF16))
        o = jnp.dot(jnp.concatenate(ps, axis=0), v_ref[0:n_keys, cols], preferred_element_type=F32)
        for g, h in enumerate(heads):
            o_ref[:, h * HEAD_DIM:(h + 1) * HEAD_DIM] = (o[g * tq:(g + 1) * tq] / ls[g]).astype(BF16)


def _dsa_prompt_kernel(*refs, tq, nq, n_sel, n_buckets):
    i = pl.program_id(1)
    per = nq // n_buckets
    for c in range(n_buckets):
        @pl.when((i >= c * per) & (i < (c + 1) * per))
        def _(c=c):
            _dsa_prompt_body(i, *refs, tq=tq, n_keys=(c + 1) * per * tq, n_sel=n_sel)


def _dsa_prompt(p, kbf, vbf, kibf, nb, t):
    tq = 128
    nq = t // tq
    n_sel = min(TOPK_MAX, t // 4)
    qrow = lambda c: (lambda b, i: (b * nq + i, c))
    return pl.pallas_call(
        functools.partial(_dsa_prompt_kernel, tq=tq, nq=nq, n_sel=n_sel, n_buckets=4),
        out_shape=jax.ShapeDtypeStruct((nb * t, D_B), BF16),
        grid=(nb, nq),
        in_specs=[
            pl.BlockSpec((tq, D_B), qrow(COL_Q // D_B)),
            pl.BlockSpec((tq, IDX_WIDTH // 2), qrow(2 * COL_QI // IDX_WIDTH)),
            pl.BlockSpec((tq, IDX_WIDTH // 2), qrow(2 * COL_QI // IDX_WIDTH + 1)),
            pl.BlockSpec((tq, LANE), qrow(COL_WI // LANE)),
            pl.BlockSpec((t, KV_WIDTH), lambda b, i: (b, 0)),
            pl.BlockSpec((t, KV_WIDTH), lambda b, i: (b, 0)),
            pl.BlockSpec((t, IDX_DIM), lambda b, i: (b, 0)),
        ],
        out_specs=pl.BlockSpec((tq, D_B), lambda b, i: (b * nq + i, 0)),
        scratch_shapes=[pltpu.VMEM((tq, t), F32)],
        compiler_params=_params("parallel", "parallel"),
        name="dsa_prompt",
    )(p, p, p, p, kbf, vbf, kibf)


def _dsa_sample_kernel(q_ref, qi0_ref, qi1_ref, wi_ref, kn_ref, vn_ref, kin_ref, kp_ref, vp_ref, kip_ref,
                       o_ref, kp_s, vp_s, kip_s, sp_ref, *, tq, n_chunks, chunk_len, n_sel):
    c = pl.program_id(1)
    off = pl.multiple_of(c * chunk_len, chunk_len)
    for h in range(N_KV_HEADS):
        kp_s[h, pl.ds(off, chunk_len), :] = kp_ref[0, 0, :, h, :].astype(BF16)
        vp_s[h, pl.ds(off, chunk_len), :] = vp_ref[0, 0, :, h, :].astype(BF16)
    kip_s[pl.ds(off, chunk_len), :] = kip_ref[0, 0].astype(BF16)

    @pl.when(c == n_chunks - 1)
    def _():
        half = IDX_HEADS // 2
        wi = wi_ref[...]
        qi_rows = jnp.concatenate(
            [(qi0_ref if h < half else qi1_ref)[:, (h % half) * IDX_DIM:(h % half + 1) * IDX_DIM]
             for h in range(IDX_HEADS)], axis=0).astype(BF16)

        def index_scores(keys):
            logits = lax.dot_general(qi_rows, keys, NT_DIMS, preferred_element_type=F32)
            total = jnp.maximum(logits[0:tq], 0.0) * wi[:, 0:1]
            for h in range(1, IDX_HEADS):
                total = total + jnp.maximum(logits[h * tq:(h + 1) * tq], 0.0) * wi[:, h:h + 1]
            return total * IDX_SCALE

        for j in range(n_chunks):
            sp_ref[:, j * chunk_len:(j + 1) * chunk_len] = index_scores(kip_s[j * chunk_len:(j + 1) * chunk_len, :])
        s_new = index_scores(kin_ref[...])

        def count_ge(key):
            cf = _key_to_float(key)
            return (jnp.sum(jnp.where(sp_ref[...] >= cf, 1.0, 0.0), axis=-1, keepdims=True)
                    + jnp.sum(jnp.where(s_new >= cf, 1.0, 0.0), axis=-1, keepdims=True))

        tau = _key_to_float(_kth_largest_key(count_ge, (tq, 1), n_sel))
        bias_p = jnp.where(sp_ref[...] >= tau, 0.0, -jnp.inf)
        bias_n = jnp.where(s_new >= tau, 0.0, -jnp.inf)

        for hk in range(N_KV_HEADS):
            heads = [hk * Q_PER_KV + g for g in range(Q_PER_KV)]
            cols = slice(hk * HEAD_DIM, (hk + 1) * HEAD_DIM)
            qs = jnp.concatenate([q_ref[:, h * HEAD_DIM:(h + 1) * HEAD_DIM] for h in heads], axis=0).astype(BF16)
            s_p = lax.dot_general(qs, kp_s[hk], NT_DIMS, preferred_element_type=F32) * ATTN_SCALE
            s_n = lax.dot_general(qs, kn_ref[:, cols], NT_DIMS, preferred_element_type=F32) * ATTN_SCALE
            pps, pns, ls = [], [], []
            for g in range(Q_PER_KV):
                rows = slice(g * tq, (g + 1) * tq)
                sp_g = s_p[rows] + bias_p
                sn_g = s_n[rows] + bias_n
                mx = jnp.maximum(jnp.max(sp_g, axis=-1, keepdims=True), jnp.max(sn_g, axis=-1, keepdims=True))
                pp = jnp.exp(sp_g - mx)
                pn = jnp.exp(sn_g - mx)
                ls.append(jnp.sum(pp, axis=-1, keepdims=True) + jnp.sum(pn, axis=-1, keepdims=True))
                pps.append(pp.astype(BF16))
                pns.append(pn.astype(BF16))
            o = (jnp.dot(jnp.concatenate(pps, axis=0), vp_s[hk], preferred_element_type=F32)
                 + jnp.dot(jnp.concatenate(pns, axis=0), vn_ref[:, cols], preferred_element_type=F32))
            for g, h in enumerate(heads):
                o_ref[:, h * HEAD_DIM:(h + 1) * HEAD_DIM] = (o[g * tq:(g + 1) * tq] / ls[g]).astype(BF16)


def _dsa_sample(p, kbf, vbf, kibf, cache_k, cache_v, cache_kidx, l, nb, t):
    assert PAST_LEN % (1 << CHUNK_SHIFT) == 0 and t <= (1 << CHUNK_SHIFT)
    n_chunks = 4
    chunk_len = PAST_LEN // n_chunks
    n_sel = min(TOPK_MAX, (PAST_LEN + t) // 4)
    qrow = lambda c: (lambda b, j: (b, c))
    new = lambda b, j: (b, 0)
    past5 = pl.BlockSpec((1, 1, chunk_len, N_KV_HEADS, HEAD_DIM), lambda b, j: (l, b, j, 0, 0))
    return pl.pallas_call(
        functools.partial(_dsa_sample_kernel, tq=t, n_chunks=n_chunks, chunk_len=chunk_len, n_sel=n_sel),
        out_shape=jax.ShapeDtypeStruct((nb * t, D_B), BF16),
        grid=(nb, n_chunks),
        in_specs=[
            pl.BlockSpec((t, D_B), qrow(COL_Q // D_B)),
            pl.BlockSpec((t, IDX_WIDTH // 2), qrow(2 * COL_QI // IDX_WIDTH)),
            pl.BlockSpec((t, IDX_WIDTH // 2), qrow(2 * COL_QI // IDX_WIDTH + 1)),
            pl.BlockSpec((t, LANE), qrow(COL_WI // LANE)),
            pl.BlockSpec((t, KV_WIDTH), new),
            pl.BlockSpec((t, KV_WIDTH), new),
            pl.BlockSpec((t, IDX_DIM), new),
            past5, past5,
            pl.BlockSpec((1, 1, chunk_len, IDX_DIM), lambda b, j: (l, b, j, 0)),
        ],
        out_specs=pl.BlockSpec((t, D_B), new),
        scratch_shapes=[pltpu.VMEM((N_KV_HEADS, PAST_LEN, HEAD_DIM), BF16),
                        pltpu.VMEM((N_KV_HEADS, PAST_LEN, HEAD_DIM), BF16),
                        pltpu.VMEM((PAST_LEN, IDX_DIM), BF16),
                        pltpu.VMEM((t, PAST_LEN), F32)],
        compiler_params=_params("parallel", "arbitrary"),
        name="dsa_sample",
    )(p, p, p, p, kbf, vbf, kibf, cache_k, cache_v, cache_kidx)


def _merge_kernel(h_ref, a_ref, b_ref, wga_ref, wgb_ref, wa_ref, wb_ref, o_ref):
    h = h_ref[...]
    ga = jax.nn.sigmoid(lax.dot_general(h, wga_ref[0], NT_DIMS, preferred_element_type=F32))
    gb = jax.nn.sigmoid(lax.dot_general(h, wgb_ref[0], NT_DIMS, preferred_element_type=F32))
    ya = jnp.dot(a_ref[...], wa_ref[0], preferred_element_type=F32)
    yb = jnp.dot(b_ref[...], wb_ref[0], preferred_element_type=F32)
    o_ref[...] = (ga * ya + gb * yb).astype(BF16)


def _merge(h, out_a, out_b, w_gates, w_pa, w_pb, l, tm):
    m = h.shape[0]
    tn = 512
    nn = D_MODEL // tn
    rows = pl.BlockSpec((tm, D_MODEL), lambda j, i: (i, 0))
    wspec = lambda off: pl.BlockSpec((1, D_MODEL, tn), lambda j, i: (l, 0, j + off))
    gspec = lambda off: pl.BlockSpec((1, tn, D_MODEL), lambda j, i: (l, j + off, 0))
    return pl.pallas_call(
        _merge_kernel,
        out_shape=jax.ShapeDtypeStruct((m, D_MODEL), BF16),
        grid=(nn, m // tm),
        in_specs=[rows, rows, rows, gspec(0), gspec(nn), wspec(0), wspec(0)],
        out_specs=pl.BlockSpec((tm, tn), lambda j, i: (i, j)),
        compiler_params=_params("parallel", "parallel"),
        name="merge",
    )(h, out_a, out_b, w_gates, w_gates, w_pa, w_pb)


def _run_trunk(x, mods, wts, nb, t, tm, tr, chunk, caches):
    sample = caches is not None
    tpg = max(t // tm, 1)
    h = _modulate(x, mods(0, 0, 0), mods(0, 0, 1), tm, tpg)
    new_k, new_v, new_ki, new_sv = [], [], [], []
    for l in range(DEPTH):
        ln = lambda s: (wts["ln_g"][l, s][None], wts["ln_b"][l, s][None])
        act = _ffn_up(h, wts["up"], l, 0, tm)
        x, h = _proj_resid_ln(act, wts["down"], l, 0, x, mods(l, 0, 2), *ln(0),
                              (mods(l, 1, 0), mods(l, 1, 1)), 0.5, tm, tpg)
        p = _in_proj(h, wts["in"], l, tm)
        outs = _gmlp(p, wts["sgu_g"][l][None], wts["sgu_b"][l][None], *wts["spatial"][l][chunk],
                     wts["ki_g"][l][None], wts["ki_b"][l][None], chunk, tr, sample)
        out_a, k4, v4, kbf, kiln, kibf, vx = outs[:7]
        if sample:
            out_b = _dsa_sample(p, kbf, vx, kibf, *caches, l, nb, t)
            new_sv.append(outs[7].reshape(nb, t, D_A))
        else:
            out_b = _dsa_prompt(p, kbf, vx, kibf, nb, t)
        z = _merge(h, out_a, out_b, wts["gates"], wts["pa"], wts["pb"], l, tm)
        x, h = _proj_resid_ln(z, wts["out"], l, 0, x, mods(l, 1, 2), *ln(1),
                              (mods(l, 2, 0), mods(l, 2, 1)), 1.0, tm, tpg)
        act = _ffn_up(h, wts["up"], l, 1, tm)
        nxt = (mods(l + 1, 0, 0), mods(l + 1, 0, 1)) if l + 1 < DEPTH else None
        x, h = _proj_resid_ln(act, wts["down"], l, 1, x, mods(l, 2, 2), *ln(2), nxt, 0.5, tm, tpg)
        new_k.append(k4.reshape(nb, t, N_KV_HEADS, HEAD_DIM))
        new_v.append(v4.reshape(nb, t, N_KV_HEADS, HEAD_DIM))
        new_ki.append(kiln.reshape(nb, t, IDX_DIM))
    return x, jnp.stack(new_k), jnp.stack(new_v), jnp.stack(new_ki), (jnp.stack(new_sv) if sample else None)


def kernel(x_prompt, x_sample, c_prompt, c_sample, cache_k, cache_v, cache_kidx, w_ada, b_ada, w_ffn_up,
           w_ffn_down, w_in, sgu_ln_g, sgu_ln_b, w_spatial, b_spatial, kidx_ln_g, kidx_ln_b, w_branch_a,
           w_branch_b, w_out, ln_g, ln_b):
    tr = 256
    c_all = jnp.concatenate([c_prompt, c_sample, jnp.zeros((ADA_ROWS - BATCH - DEC_BATCH, D_MODEL), F32)], axis=0)
    ada = _ada(c_all, w_ada, b_ada)

    def mod_cols(l, sub, kind, rows):
        c0 = (sub * 3 + kind) * D_MODEL
        return ada[l, rows, c0:c0 + D_MODEL]

    def mods_prompt(l, sub, kind):
        return mod_cols(l, sub, kind, slice(0, BATCH))[:, None, :]

    def mods_sample(l, sub, kind):
        rows = mod_cols(l, sub, kind, slice(BATCH, BATCH + DEC_BATCH))
        return jnp.repeat(rows, DEC_SEQ, axis=0)[None]

    def spatial(l, chunk):
        reps = tr // chunk
        w = jnp.tile(w_spatial[l][:, :chunk, :chunk], (1, reps, reps))
        b = jnp.tile(jnp.repeat(jnp.transpose(b_spatial[l][:, :chunk]), GMLP_GDIM, axis=1), (reps, 1))
        return w, b

    w_in_t = jnp.swapaxes(w_in, 1, 2)
    wts = dict(
        up=w_ffn_up, **{"in": w_in_t},
        down=w_ffn_down.astype(BF16),
        out=w_out.astype(BF16)[:, None],
        pa=w_branch_a.astype(BF16), pb=w_branch_b.astype(BF16),
        gates=w_in_t[:, COL_GATES:COL_GATES + 2 * D_MODEL, :].astype(BF16),
        ln_g=ln_g, ln_b=ln_b, sgu_g=sgu_ln_g, sgu_b=sgu_ln_b, ki_g=kidx_ln_g, ki_b=kidx_ln_b,
        spatial=[{c: spatial(l, c) for c in (GMLP_CHUNK, DEC_SEQ)} for l in range(DEPTH)],
    )

    y_p, k_p, v_p, ki_p, _ = _run_trunk(x_prompt.reshape(BATCH * SEQ, D_MODEL), mods_prompt, wts,
                                        BATCH, SEQ, 512, tr, GMLP_CHUNK, None)
    m_s = DEC_BATCH * DEC_SEQ
    y_s, k_s, v_s, ki_s, sv_s = _run_trunk(x_sample.reshape(m_s, D_MODEL), mods_sample, wts,
                                           DEC_BATCH, DEC_SEQ, m_s, tr, DEC_SEQ, (cache_k, cache_v, cache_kidx))
    return (y_p.reshape(BATCH, SEQ, D_MODEL), y_s.reshape(DEC_BATCH, DEC_SEQ, D_MODEL),
            k_p, v_p, ki_p, k_s, v_s, ki_s, sv_s)
```

```python
import functools

import jax
import jax.numpy as jnp
from jax import lax
from jax.experimental import pallas as pl
from jax.experimental.pallas import tpu as pltpu

F32 = jnp.float32
BF16 = jnp.bfloat16

D_MODEL = 2048
BATCH = 4
SEQ = 2048
DEPTH = 4
DEC_BATCH = 8
DEC_SEQ = 32
PAST_LEN = 4096
CHUNK_SHIFT = 6
GMLP_CHUNK = 128
GMLP_GROUPS = 8
D_A = D_MODEL
GMLP_GDIM = D_A // GMLP_GROUPS
HEAD_DIM = 128
N_HEADS = D_MODEL // HEAD_DIM
N_KV_HEADS = 4
Q_PER_KV = N_HEADS // N_KV_HEADS
D_B = N_HEADS * HEAD_DIM
KV_WIDTH = N_KV_HEADS * HEAD_DIM
IDX_HEADS = 16
IDX_DIM = 128
IDX_WIDTH = IDX_HEADS * IDX_DIM
TOPK_MAX = 256
D_FF = 5632
ALPHA = (2 * DEPTH) ** 0.25
LN_EPS = 1e-5
IDX_SCALE = (IDX_HEADS * IDX_DIM) ** -0.5
ATTN_SCALE = HEAD_DIM ** -0.5
LOG2_E = 1.4426950408889634
IDX_KEY_CHUNK = 512
ADA_WIDTH = 9 * D_MODEL
ADA_ROWS = 16

COL_U = 0
COL_SV = D_A
COL_Q = 2 * D_A
COL_K = COL_Q + D_B
COL_V = COL_K + KV_WIDTH
COL_QI = COL_V + KV_WIDTH
COL_KI = COL_QI + IDX_WIDTH
COL_WI = COL_KI + IDX_DIM
COL_GATES = COL_WI + IDX_HEADS
IN_TN = 1024
IN_MAIN_WIDTH = 10 * IN_TN

LANE = 128
VMEM_LIMIT = 56 * 1024 * 1024
NT_DIMS = (((1,), (1,)), ((), ()))


def _params(*sem):
    return pltpu.CompilerParams(dimension_semantics=sem, vmem_limit_bytes=VMEM_LIMIT)


def _layer_norm_rows(r, g, b):
    mu = jnp.mean(r, axis=-1, keepdims=True)
    rc = r - mu
    var = jnp.mean(rc * rc, axis=-1, keepdims=True)
    return rc * lax.rsqrt(var + LN_EPS) * g + b


def _key_to_float(key):
    return lax.bitcast_convert_type(key ^ ((key >> 31) & 0x7FFFFFFF), F32)


def _kth_largest_key(count_ge, shape, n_sel):
    zero = jnp.zeros(shape, jnp.int32)
    prefix = jnp.where(count_ge(zero) >= n_sel, zero, jnp.int32(-2 ** 31))
    for bit in range(30, -1, -1):
        cand = prefix | jnp.int32(1 << bit)
        prefix = jnp.where(count_ge(cand) >= n_sel, cand, prefix)
    return prefix


def _ada_kernel(c_ref, w_ref, b_ref, o_ref):
    c = c_ref[...]
    s = (c * jax.nn.sigmoid(c)).astype(BF16)
    o_ref[0] = jnp.dot(s, w_ref[0].astype(BF16), preferred_element_type=F32) + b_ref[0]


def _ada(c_all, w_ada, b_ada):
    tn = 1024
    return pl.pallas_call(
        _ada_kernel,
        out_shape=jax.ShapeDtypeStruct((DEPTH, ADA_ROWS, ADA_WIDTH), F32),
        grid=(DEPTH, ADA_WIDTH // tn),
        in_specs=[
            pl.BlockSpec((ADA_ROWS, D_MODEL), lambda l, j: (0, 0)),
            pl.BlockSpec((1, D_MODEL, tn), lambda l, j: (l, 0, j)),
            pl.BlockSpec((1, 1, tn), lambda l, j: (l, 0, j)),
        ],
        out_specs=pl.BlockSpec((1, ADA_ROWS, tn), lambda l, j: (l, 0, j)),
        compiler_params=_params("parallel", "parallel"),
        name="ada",
    )(c_all, w_ada, b_ada.reshape(DEPTH, 1, ADA_WIDTH))


def _mod_kernel(x_ref, sh_ref, sc_ref, h_ref):
    h_ref[...] = (x_ref[...] * (1.0 + sc_ref[0]) + sh_ref[0]).astype(BF16)


def _mod_spec(mod, tiles_per_group):
    return pl.BlockSpec((1, mod.shape[1], D_MODEL), lambda i, *_: (i // tiles_per_group, 0, 0))


def _modulate(x, shift, scale, tm, tpg):
    m = x.shape[0]
    return pl.pallas_call(
        _mod_kernel,
        out_shape=jax.ShapeDtypeStruct((m, D_MODEL), BF16),
        grid=(m // tm,),
        in_specs=[pl.BlockSpec((tm, D_MODEL), lambda i: (i, 0)), _mod_spec(shift, tpg), _mod_spec(scale, tpg)],
        out_specs=pl.BlockSpec((tm, D_MODEL), lambda i: (i, 0)),
        compiler_params=_params("parallel"),
        name="modulate",
    )(x, shift, scale)


def _up_kernel(h_ref, wa_ref, wb_ref, o_ref, wa_s, wb_s):
    @pl.when(pl.program_id(1) == 0)
    def _():
        wa_s[...] = wa_ref[0, 0].astype(BF16)
        wb_s[...] = wb_ref[0, 0].astype(BF16)

    h = h_ref[...]
    a = jnp.dot(h, wa_s[...], preferred_element_type=F32)
    b = jnp.dot(h, wb_s[...], preferred_element_type=F32)
    o_ref[...] = (a * jax.nn.sigmoid(a) * b).astype(BF16)


def _ffn_up(h, w_up, l, s, tm):
    m = h.shape[0]
    tf = 512
    nf = D_FF // tf
    return pl.pallas_call(
        _up_kernel,
        out_shape=jax.ShapeDtypeStruct((m, D_FF), BF16),
        grid=(nf, m // tm),
        in_specs=[
            pl.BlockSpec((tm, D_MODEL), lambda j, i: (i, 0)),
            pl.BlockSpec((1, 1, D_MODEL, tf), lambda j, i: (l, s, 0, j)),
            pl.BlockSpec((1, 1, D_MODEL, tf), lambda j, i: (l, s, 0, j + nf)),
        ],
        out_specs=pl.BlockSpec((tm, tf), lambda j, i: (i, j)),
        scratch_shapes=[pltpu.VMEM((D_MODEL, tf), BF16), pltpu.VMEM((D_MODEL, tf), BF16)],
        compiler_params=_params("parallel", "arbitrary"),
        name="ffn_up",
    )(h, w_up, w_up)


def _down_kernel(*refs, gscale, nk, emit_h):
    if emit_h:
        a_ref, w_ref, x_ref, g_ref, lg_ref, lb_ref, sh_ref, sc_ref, xo_ref, ho_ref, acc_ref = refs
    else:
        a_ref, w_ref, x_ref, g_ref, lg_ref, lb_ref, xo_ref, acc_ref = refs
    k = pl.program_id(1)
    part = jnp.dot(a_ref[...], w_ref[0, 0], preferred_element_type=F32)

    def finish(y):
        r = ALPHA * x_ref[...] + (gscale * g_ref[0]) * y
        xn = _layer_norm_rows(r, lg_ref[...], lb_ref[...])
        xo_ref[...] = xn
        if emit_h:
            ho_ref[...] = (xn * (1.0 + sc_ref[0]) + sh_ref[0]).astype(BF16)

    if nk == 1:
        finish(part)
        return

    @pl.when(k == 0)
    def _():
        acc_ref[...] = part

    @pl.when((k > 0) & (k < nk - 1))
    def _():
        acc_ref[...] += part

    @pl.when(k == nk - 1)
    def _():
        finish(acc_ref[...] + part)


def _proj_resid_ln(a, w, l, s, x, gate, lg, lb, nxt, gscale, tm, tpg):
    m, kdim = a.shape
    tk = 1408 if kdim == D_FF else kdim
    nk = kdim // tk
    emit_h = nxt is not None
    row = lambda i, k: (i, 0)
    const = lambda i, k: (0, 0)
    in_specs = [
        pl.BlockSpec((tm, tk), lambda i, k: (i, k)),
        pl.BlockSpec((1, 1, tk, D_MODEL), lambda i, k: (l, s, k, 0)),
        pl.BlockSpec((tm, D_MODEL), row),
        _mod_spec(gate, tpg),
        pl.BlockSpec((1, D_MODEL), const),
        pl.BlockSpec((1, D_MODEL), const),
    ]
    args = [a, w, x, gate, lg, lb]
    out_shape = [jax.ShapeDtypeStruct((m, D_MODEL), F32)]
    out_specs = [pl.BlockSpec((tm, D_MODEL), row)]
    if emit_h:
        in_specs += [_mod_spec(nxt[0], tpg), _mod_spec(nxt[1], tpg)]
        args += [nxt[0], nxt[1]]
        out_shape.append(jax.ShapeDtypeStruct((m, D_MODEL), BF16))
        out_specs.append(pl.BlockSpec((tm, D_MODEL), row))
    outs = pl.pallas_call(
        functools.partial(_down_kernel, gscale=gscale, nk=nk, emit_h=emit_h),
        out_shape=out_shape,
        grid=(m // tm, nk),
        in_specs=in_specs,
        out_specs=out_specs,
        scratch_shapes=[pltpu.VMEM((tm, D_MODEL) if nk > 1 else (8, LANE), F32)],
        compiler_params=_params("parallel", "arbitrary"),
        name="proj_resid_ln",
    )(*args)
    return (outs[0], outs[1]) if emit_h else (outs[0], None)


def _in_proj_kernel(h_ref, w_ref, o_ref, w_s):
    @pl.when(pl.program_id(1) == 0)
    def _():
        w_s[...] = w_ref[0].T.astype(BF16)

    o_ref[...] = jnp.dot(h_ref[...], w_s[...], preferred_element_type=F32)


def _in_proj(h, w_in_t, l, tm):
    m = h.shape[0]
    return pl.pallas_call(
        _in_proj_kernel,
        out_shape=jax.ShapeDtypeStruct((m, IN_MAIN_WIDTH), F32),
        grid=(IN_MAIN_WIDTH // IN_TN, m // tm),
        in_specs=[pl.BlockSpec((tm, D_MODEL), lambda j, i: (i, 0)),
                  pl.BlockSpec((1, IN_TN, D_MODEL), lambda j, i: (l, j, 0))],
        out_specs=pl.BlockSpec((tm, IN_TN), lambda j, i: (i, j)),
        scratch_shapes=[pltpu.VMEM((D_MODEL, IN_TN), BF16)],
        compiler_params=_params("parallel", "arbitrary"),
        name="in_proj",
    )(h, w_in_t)


def _gmlp_kernel(*refs, tr, chunk, sample):
    (u_ref, sv_ref, k_ref, v_ref, ki_ref, sg_ref, sb_ref, ws_ref, bias_ref, kg_ref, kb_ref,
     oa_ref, k4_ref, v4_ref, kbf_ref, kiln_ref, kibf_ref, vx_ref) = refs[:18]
    u = jax.nn.gelu(u_ref[...])
    svn = _layer_norm_rows(jax.nn.gelu(sv_ref[...]), sg_ref[...], sb_ref[...])
    if sample:
        refs[18][...] = svn
    svb = svn.astype(BF16)
    row = lax.broadcasted_iota(jnp.int32, (tr, tr), 0)
    col = lax.broadcasted_iota(jnp.int32, (tr, tr), 1)
    shift = chunk.bit_length() - 1
    visible = ((row >> shift) == (col >> shift)) & (col <= row)
    for g in range(GMLP_GROUPS):
        cols = slice(g * GMLP_GDIM, (g + 1) * GMLP_GDIM)
        wm = jnp.where(visible, ws_ref[g], 0.0).astype(BF16)
        f = jnp.dot(wm, svb[:, cols], preferred_element_type=F32) + bias_ref[:, cols]
        oa_ref[:, cols] = (u[:, cols] * f).astype(BF16)
    k = k_ref[...]
    v = v_ref[...]
    for h in range(N_KV_HEADS):
        k4_ref[:, h, :] = k[:, h * HEAD_DIM:(h + 1) * HEAD_DIM]
        v4_ref[:, h, :] = v[:, h * HEAD_DIM:(h + 1) * HEAD_DIM]
    kbf_ref[...] = k.astype(BF16)
    vx_ref[...] = v.astype(BF16)
    kiln = _layer_norm_rows(ki_ref[...], kg_ref[...], kb_ref[...])
    kiln_ref[...] = kiln
    kibf_ref[...] = kiln.astype(BF16)


def _gmlp(p, sgu_g, sgu_b, w_tiled, bias_tiled, ki_g, ki_b, chunk, tr, sample):
    m = p.shape[0]
    row = lambda c: (lambda i: (i, c))
    const2 = lambda i: (0, 0)
    out_shape = [
        jax.ShapeDtypeStruct((m, D_A), BF16),
        jax.ShapeDtypeStruct((m, N_KV_HEADS, HEAD_DIM), F32),
        jax.ShapeDtypeStruct((m, N_KV_HEADS, HEAD_DIM), F32),
        jax.ShapeDtypeStruct((m, KV_WIDTH), BF16),
        jax.ShapeDtypeStruct((m, IDX_DIM), F32),
        jax.ShapeDtypeStruct((m, IDX_DIM), BF16),
        jax.ShapeDtypeStruct((m, KV_WIDTH), BF16),
    ]
    head4 = pl.BlockSpec((tr, N_KV_HEADS, HEAD_DIM), lambda i: (i, 0, 0))
    out_specs = [
        pl.BlockSpec((tr, D_A), row(0)), head4, head4,
        pl.BlockSpec((tr, KV_WIDTH), row(0)),
        pl.BlockSpec((tr, IDX_DIM), row(0)),
        pl.BlockSpec((tr, IDX_DIM), row(0)),
        pl.BlockSpec((tr, KV_WIDTH), row(0)),
    ]
    if sample:
        out_shape.append(jax.ShapeDtypeStruct((m, D_A), F32))
        out_specs.append(pl.BlockSpec((tr, D_A), row(0)))
    return pl.pallas_call(
        functools.partial(_gmlp_kernel, tr=tr, chunk=chunk, sample=sample),
        out_shape=out_shape,
        grid=(m // tr,),
        in_specs=[
            pl.BlockSpec((tr, D_A), row(COL_U // D_A)),
            pl.BlockSpec((tr, D_A), row(COL_SV // D_A)),
            pl.BlockSpec((tr, KV_WIDTH), row(COL_K // KV_WIDTH)),
            pl.BlockSpec((tr, KV_WIDTH), row(COL_V // KV_WIDTH)),
            pl.BlockSpec((tr, IDX_DIM), row(COL_KI // IDX_DIM)),
            pl.BlockSpec((1, D_A), const2),
            pl.BlockSpec((1, D_A), const2),
            pl.BlockSpec((GMLP_GROUPS, tr, tr), lambda i: (0, 0, 0)),
            pl.BlockSpec((tr, D_A), const2),
            pl.BlockSpec((1, IDX_DIM), const2),
            pl.BlockSpec((1, IDX_DIM), const2),
        ],
        out_specs=out_specs,
        compiler_params=_params("parallel"),
        name="gmlp",
    )(p, p, p, p, p, sgu_g, sgu_b, w_tiled, bias_tiled, ki_g, ki_b)


def _dsa_prompt_body(i, q_ref, qi0_ref, qi1_ref, wi_ref, k_ref, v_ref, ki_ref, o_ref, score_ref, score_t_ref,
                     *, tq, n_keys, n_sel):
    half = IDX_HEADS // 2
    wi = wi_ref[...]
    qi_rows = jnp.concatenate(
        [(qi0_ref if h < half else qi1_ref)[:, (h % half) * IDX_DIM:(h % half + 1) * IDX_DIM]
         for h in range(IDX_HEADS)], axis=0).astype(BF16)
    for c0 in range(0, n_keys, IDX_KEY_CHUNK):
        logits = lax.dot_general(qi_rows, ki_ref[c0:c0 + IDX_KEY_CHUNK, :], NT_DIMS, preferred_element_type=F32)
        total = jnp.maximum(logits[0:tq], 0.0) * wi[:, 0:1]
        for h in range(1, IDX_HEADS):
            total = total + jnp.maximum(logits[h * tq:(h + 1) * tq], 0.0) * wi[:, h:h + 1]
        score_ref[:, c0:c0 + IDX_KEY_CHUNK] = total * IDX_SCALE
    qpos = i * tq + lax.broadcasted_iota(jnp.int32, (tq, n_keys), 0)
    kpos = lax.broadcasted_iota(jnp.int32, (tq, n_keys), 1)
    admissible = (kpos >> CHUNK_SHIFT) <= (qpos >> CHUNK_SHIFT)
    score_ref[:, 0:n_keys] = jnp.where(admissible, score_ref[:, 0:n_keys], -jnp.inf)
    n_adm = jnp.sum(jnp.where(admissible, 1.0, 0.0), axis=-1, keepdims=True)

    for c0 in range(0, n_keys, tq):
        score_t_ref[c0:c0 + tq, :] = score_ref[:, c0:c0 + tq].T

    def count_ge(key):
        cf = _key_to_float(key)
        part = jnp.where(score_t_ref[0:tq, :] >= cf, 1.0, 0.0)
        for r0 in range(tq, n_keys, tq):
            part = part + jnp.where(score_t_ref[r0:r0 + tq, :] >= cf, 1.0, 0.0)
        return jnp.sum(part, axis=0, keepdims=True)

    prefix = _kth_largest_key(count_ge, (1, tq), n_sel)
    tau_col = jnp.broadcast_to(_key_to_float(prefix), (tq, tq)).T[:, 0:1]
    tau = jnp.where(n_adm <= n_sel, -jnp.inf, tau_col)
    bias = jnp.where(admissible, jnp.where(score_ref[:, 0:n_keys] >= tau, 0.0, -jnp.inf), -jnp.inf)

    for hk in range(N_KV_HEADS):
        heads = [hk * Q_PER_KV + g for g in range(Q_PER_KV)]
        cols = slice(hk * HEAD_DIM, (hk + 1) * HEAD_DIM)
        qs = jnp.concatenate([q_ref[:, h * HEAD_DIM:(h + 1) * HEAD_DIM] for h in heads], axis=0)
        qs = (qs * (ATTN_SCALE * LOG2_E)).astype(BF16)
        s = lax.dot_general(qs, k_ref[0:n_keys, cols], NT_DIMS, preferred_element_type=F32)
        ps, ls = [], []
        for g in range(Q_PER_KV):
            sg = s[g * tq:(g + 1) * tq] + bias
            pg = jnp.exp2(sg - jnp.max(sg, axis=-1, keepdims=True))
            ls.append(jnp.sum(pg, axis=-1, keepdims=True))
            ps.append(pg.astype(BF16))
        o = jnp.dot(jnp.concatenate(ps, axis=0), v_ref[0:n_keys, cols], preferred_element_type=F32)
        for g, h in enumerate(heads):
            o_ref[:, h * HEAD_DIM:(h + 1) * HEAD_DIM] = (o[g * tq:(g + 1) * tq] / ls[g]).astype(BF16)


def _dsa_prompt_kernel(*refs, tq, nq, n_sel, n_buckets):
    i = pl.program_id(1)
    per = nq // n_buckets
    for c in range(n_buckets):
        @pl.when((i >= c * per) & (i < (c + 1) * per))
        def _(c=c):
            _dsa_prompt_body(i, *refs, tq=tq, n_keys=(c + 1) * per * tq, n_sel=n_sel)


def _dsa_prompt(p, kbf, vbf, kibf, nb, t):
    tq = 128
    nq = t // tq
    n_sel = min(TOPK_MAX, t // 4)
    qrow = lambda c: (lambda b, i: (b * nq + i, c))
    return pl.pallas_call(
        functools.partial(_dsa_prompt_kernel, tq=tq, nq=nq, n_sel=n_sel, n_buckets=4),
        out_shape=jax.ShapeDtypeStruct((nb * t, D_B), BF16),
        grid=(nb, nq),
        in_specs=[
            pl.BlockSpec((tq, D_B), qrow(COL_Q // D_B)),
            pl.BlockSpec((tq, IDX_WIDTH // 2), qrow(2 * COL_QI // IDX_WIDTH)),
            pl.BlockSpec((tq, IDX_WIDTH // 2), qrow(2 * COL_QI // IDX_WIDTH + 1)),
            pl.BlockSpec((tq, LANE), qrow(COL_WI // LANE)),
            pl.BlockSpec((t, KV_WIDTH), lambda b, i: (b, 0)),
            pl.BlockSpec((t, KV_WIDTH), lambda b, i: (b, 0)),
            pl.BlockSpec((t, IDX_DIM), lambda b, i: (b, 0)),
        ],
        out_specs=pl.BlockSpec((tq, D_B), lambda b, i: (b * nq + i, 0)),
        scratch_shapes=[pltpu.VMEM((tq, t), F32), pltpu.VMEM((t, tq), F32)],
        compiler_params=_params("parallel", "parallel"),
        name="dsa_prompt",
    )(p, p, p, p, kbf, vbf, kibf)


def _dsa_sample_kernel(q_ref, qi0_ref, qi1_ref, wi_ref, kn_ref, vn_ref, kin_ref, kp_ref, vp_ref, kip_ref,
                       o_ref, kp_s, vp_s, kip_s, sp_ref, *, tq, n_chunks, chunk_len, n_sel):
    c = pl.program_id(1)
    off = pl.multiple_of(c * chunk_len, chunk_len)
    for h in range(N_KV_HEADS):
        kp_s[h, pl.ds(off, chunk_len), :] = kp_ref[0, 0, pl.ds(h, chunk_len, stride=N_KV_HEADS), :].astype(BF16)
        vp_s[h, pl.ds(off, chunk_len), :] = vp_ref[0, 0, pl.ds(h, chunk_len, stride=N_KV_HEADS), :].astype(BF16)
    kip_s[pl.ds(off, chunk_len), :] = kip_ref[0, 0].astype(BF16)

    @pl.when(c == n_chunks - 1)
    def _():
        half = IDX_HEADS // 2
        wi = wi_ref[...]
        qi_rows = jnp.concatenate(
            [(qi0_ref if h < half else qi1_ref)[:, (h % half) * IDX_DIM:(h % half + 1) * IDX_DIM]
             for h in range(IDX_HEADS)], axis=0).astype(BF16)

        def index_scores(keys):
            logits = lax.dot_general(qi_rows, keys, NT_DIMS, preferred_element_type=F32)
            total = jnp.maximum(logits[0:tq], 0.0) * wi[:, 0:1]
            for h in range(1, IDX_HEADS):
                total = total + jnp.maximum(logits[h * tq:(h + 1) * tq], 0.0) * wi[:, h:h + 1]
            return total * IDX_SCALE

        for j in range(n_chunks):
            sp_ref[:, j * chunk_len:(j + 1) * chunk_len] = index_scores(kip_s[j * chunk_len:(j + 1) * chunk_len, :])
        s_new = index_scores(kin_ref[...])

        def count_ge(key):
            cf = _key_to_float(key)
            return (jnp.sum(jnp.where(sp_ref[...] >= cf, 1.0, 0.0), axis=-1, keepdims=True)
                    + jnp.sum(jnp.where(s_new >= cf, 1.0, 0.0), axis=-1, keepdims=True))

        tau = _key_to_float(_kth_largest_key(count_ge, (tq, 1), n_sel))
        bias_p = jnp.where(sp_ref[...] >= tau, 0.0, -jnp.inf)
        bias_n = jnp.where(s_new >= tau, 0.0, -jnp.inf)

        for hk in range(N_KV_HEADS):
            heads = [hk * Q_PER_KV + g for g in range(Q_PER_KV)]
            cols = slice(hk * HEAD_DIM, (hk + 1) * HEAD_DIM)
            qs = jnp.concatenate([q_ref[:, h * HEAD_DIM:(h + 1) * HEAD_DIM] for h in heads], axis=0).astype(BF16)
            s_p = lax.dot_general(qs, kp_s[hk], NT_DIMS, preferred_element_type=F32) * ATTN_SCALE
            s_n = lax.dot_general(qs, kn_ref[:, cols], NT_DIMS, preferred_element_type=F32) * ATTN_SCALE
            pps, pns, ls = [], [], []
            for g in range(Q_PER_KV):
                rows = slice(g * tq, (g + 1) * tq)
                sp_g = s_p[rows] + bias_p
                sn_g = s_n[rows] + bias_n
                mx = jnp.maximum(jnp.max(sp_g, axis=-1, keepdims=True), jnp.max(sn_g, axis=-1, keepdims=True))
                pp = jnp.exp(sp_g - mx)
                pn = jnp.exp(sn_g - mx)
                ls.append(jnp.sum(pp, axis=-1, keepdims=True) + jnp.sum(pn, axis=-1, keepdims=True))
                pps.append(pp.astype(BF16))
                pns.append(pn.astype(BF16))
            o = (jnp.dot(jnp.concatenate(pps, axis=0), vp_s[hk], preferred_element_type=F32)
                 + jnp.dot(jnp.concatenate(pns, axis=0), vn_ref[:, cols], preferred_element_type=F32))
            for g, h in enumerate(heads):
                o_ref[:, h * HEAD_DIM:(h + 1) * HEAD_DIM] = (o[g * tq:(g + 1) * tq] / ls[g]).astype(BF16)


def _dsa_sample(p, kbf, vbf, kibf, cache_k, cache_v, cache_kidx, l, nb, t):
    assert PAST_LEN % (1 << CHUNK_SHIFT) == 0 and t <= (1 << CHUNK_SHIFT)
    n_chunks = 4
    chunk_len = PAST_LEN // n_chunks
    n_sel = min(TOPK_MAX, (PAST_LEN + t) // 4)
    qrow = lambda c: (lambda b, j: (b, c))
    new = lambda b, j: (b, 0)
    past5 = pl.BlockSpec((1, 1, chunk_len * N_KV_HEADS, HEAD_DIM), lambda b, j: (l, b, j, 0))
    rows_per_batch = PAST_LEN * N_KV_HEADS
    cache_k = cache_k.reshape(DEPTH, nb, rows_per_batch, HEAD_DIM)
    cache_v = cache_v.reshape(DEPTH, nb, rows_per_batch, HEAD_DIM)
    return pl.pallas_call(
        functools.partial(_dsa_sample_kernel, tq=t, n_chunks=n_chunks, chunk_len=chunk_len, n_sel=n_sel),
        out_shape=jax.ShapeDtypeStruct((nb * t, D_B), BF16),
        grid=(nb, n_chunks),
        in_specs=[
            pl.BlockSpec((t, D_B), qrow(COL_Q // D_B)),
            pl.BlockSpec((t, IDX_WIDTH // 2), qrow(2 * COL_QI // IDX_WIDTH)),
            pl.BlockSpec((t, IDX_WIDTH // 2), qrow(2 * COL_QI // IDX_WIDTH + 1)),
            pl.BlockSpec((t, LANE), qrow(COL_WI // LANE)),
            pl.BlockSpec((t, KV_WIDTH), new),
            pl.BlockSpec((t, KV_WIDTH), new),
            pl.BlockSpec((t, IDX_DIM), new),
            past5, past5,
            pl.BlockSpec((1, 1, chunk_len, IDX_DIM), lambda b, j: (l, b, j, 0)),
        ],
        out_specs=pl.BlockSpec((t, D_B), new),
        scratch_shapes=[pltpu.VMEM((N_KV_HEADS, PAST_LEN, HEAD_DIM), BF16),
                        pltpu.VMEM((N_KV_HEADS, PAST_LEN, HEAD_DIM), BF16),
                        pltpu.VMEM((PAST_LEN, IDX_DIM), BF16),
                        pltpu.VMEM((t, PAST_LEN), F32)],
        compiler_params=_params("parallel", "arbitrary"),
        name="dsa_sample",
    )(p, p, p, p, kbf, vbf, kibf, cache_k, cache_v, cache_kidx)


def _merge_kernel(h_ref, a_ref, b_ref, wga_ref, wgb_ref, wa_ref, wb_ref, o_ref):
    h = h_ref[...]
    ga = jax.nn.sigmoid(lax.dot_general(h, wga_ref[0], NT_DIMS, preferred_element_type=F32))
    gb = jax.nn.sigmoid(lax.dot_general(h, wgb_ref[0], NT_DIMS, preferred_element_type=F32))
    ya = jnp.dot(a_ref[...], wa_ref[0], preferred_element_type=F32)
    yb = jnp.dot(b_ref[...], wb_ref[0], preferred_element_type=F32)
    o_ref[...] = (ga * ya + gb * yb).astype(BF16)


def _merge(h, out_a, out_b, w_gates, w_pa, w_pb, l, tm):
    m = h.shape[0]
    tn = 512
    nn = D_MODEL // tn
    rows = pl.BlockSpec((tm, D_MODEL), lambda j, i: (i, 0))
    wspec = lambda off: pl.BlockSpec((1, D_MODEL, tn), lambda j, i: (l, 0, j + off))
    gspec = lambda off: pl.BlockSpec((1, tn, D_MODEL), lambda j, i: (l, j + off, 0))
    return pl.pallas_call(
        _merge_kernel,
        out_shape=jax.ShapeDtypeStruct((m, D_MODEL), BF16),
        grid=(nn, m // tm),
        in_specs=[rows, rows, rows, gspec(0), gspec(nn), wspec(0), wspec(0)],
        out_specs=pl.BlockSpec((tm, tn), lambda j, i: (i, j)),
        compiler_params=_params("parallel", "parallel"),
        name="merge",
    )(h, out_a, out_b, w_gates, w_gates, w_pa, w_pb)


def _run_trunk(x, mods, wts, nb, t, tm, tr, chunk, caches):
    sample = caches is not None
    tm_wide = min(2 * tm, x.shape[0])
    tpg = max(t // tm, 1)
    h = _modulate(x, mods(0, 0, 0), mods(0, 0, 1), tm, tpg)
    new_k, new_v, new_ki, new_sv = [], [], [], []
    for l in range(DEPTH):
        ln = lambda s: (wts["ln_g"][l, s][None], wts["ln_b"][l, s][None])
        act = _ffn_up(h, wts["up"], l, 0, tm_wide)
        x, h = _proj_resid_ln(act, wts["down"], l, 0, x, mods(l, 0, 2), *ln(0),
                              (mods(l, 1, 0), mods(l, 1, 1)), 0.5, tm, tpg)
        p = _in_proj(h, wts["in"], l, tm_wide)
        outs = _gmlp(p, wts["sgu_g"][l][None], wts["sgu_b"][l][None], *wts["spatial"][l][chunk],
                     wts["ki_g"][l][None], wts["ki_b"][l][None], chunk, tr, sample)
        out_a, k4, v4, kbf, kiln, kibf, vx = outs[:7]
        if sample:
            out_b = _dsa_sample(p, kbf, vx, kibf, *caches, l, nb, t)
            new_sv.append(outs[7].reshape(nb, t, D_A))
        else:
            out_b = _dsa_prompt(p, kbf, vx, kibf, nb, t)
        z = _merge(h, out_a, out_b, wts["gates"], wts["pa"], wts["pb"], l, tm)
        x, h = _proj_resid_ln(z, wts["out"], l, 0, x, mods(l, 1, 2), *ln(1),
                              (mods(l, 2, 0), mods(l, 2, 1)), 1.0, tm, tpg)
        act = _ffn_up(h, wts["up"], l, 1, tm_wide)
        nxt = (mods(l + 1, 0, 0), mods(l + 1, 0, 1)) if l + 1 < DEPTH else None
        x, h = _proj_resid_ln(act, wts["down"], l, 1, x, mods(l, 2, 2), *ln(2), nxt, 0.5, tm, tpg)
        new_k.append(k4.reshape(nb, t, N_KV_HEADS, HEAD_DIM))
        new_v.append(v4.reshape(nb, t, N_KV_HEADS, HEAD_DIM))
        new_ki.append(kiln.reshape(nb, t, IDX_DIM))
    return x, jnp.stack(new_k), jnp.stack(new_v), jnp.stack(new_ki), (jnp.stack(new_sv) if sample else None)


def kernel(x_prompt, x_sample, c_prompt, c_sample, cache_k, cache_v, cache_kidx, w_ada, b_ada, w_ffn_up,
           w_ffn_down, w_in, sgu_ln_g, sgu_ln_b, w_spatial, b_spatial, kidx_ln_g, kidx_ln_b, w_branch_a,
           w_branch_b, w_out, ln_g, ln_b):
    tr = 256
    c_all = jnp.concatenate([c_prompt, c_sample, jnp.zeros((ADA_ROWS - BATCH - DEC_BATCH, D_MODEL), F32)], axis=0)
    ada = _ada(c_all, w_ada, b_ada)

    def mod_cols(l, sub, kind, rows):
        c0 = (sub * 3 + kind) * D_MODEL
        return ada[l, rows, c0:c0 + D_MODEL]

    def mods_prompt(l, sub, kind):
        return mod_cols(l, sub, kind, slice(0, BATCH))[:, None, :]

    def mods_sample(l, sub, kind):
        rows = mod_cols(l, sub, kind, slice(BATCH, BATCH + DEC_BATCH))
        return jnp.repeat(rows, DEC_SEQ, axis=0)[None]

    def spatial(l, chunk):
        reps = tr // chunk
        w = jnp.tile(w_spatial[l][:, :chunk, :chunk], (1, reps, reps))
        b = jnp.tile(jnp.repeat(jnp.transpose(b_spatial[l][:, :chunk]), GMLP_GDIM, axis=1), (reps, 1))
        return w, b

    w_in_t = jnp.swapaxes(w_in, 1, 2)
    wts = dict(
        up=w_ffn_up, **{"in": w_in_t},
        down=w_ffn_down.astype(BF16),
        out=w_out.astype(BF16)[:, None],
        pa=w_branch_a.astype(BF16), pb=w_branch_b.astype(BF16),
        gates=w_in_t[:, COL_GATES:COL_GATES + 2 * D_MODEL, :].astype(BF16),
        ln_g=ln_g, ln_b=ln_b, sgu_g=sgu_ln_g, sgu_b=sgu_ln_b, ki_g=kidx_ln_g, ki_b=kidx_ln_b,
        spatial=[{c: spatial(l, c) for c in (GMLP_CHUNK, DEC_SEQ)} for l in range(DEPTH)],
    )

    y_p, k_p, v_p, ki_p, _ = _run_trunk(x_prompt.reshape(BATCH * SEQ, D_MODEL), mods_prompt, wts,
                                        BATCH, SEQ, 512, tr, GMLP_CHUNK, None)
    m_s = DEC_BATCH * DEC_SEQ
    y_s, k_s, v_s, ki_s, sv_s = _run_trunk(x_sample.reshape(m_s, D_MODEL), mods_sample, wts,
                                           DEC_BATCH, DEC_SEQ, m_s, tr, DEC_SEQ, (cache_k, cache_v, cache_kidx))
    return (y_p.reshape(BATCH, SEQ, D_MODEL), y_s.reshape(DEC_BATCH, DEC_SEQ, D_MODEL),
            k_p, v_p, ki_p, k_s, v_s, ki_s, sv_s)
```

```python
import functools

import jax
import jax.numpy as jnp
from jax import lax
from jax.experimental import pallas as pl
from jax.experimental.pallas import tpu as pltpu

F32 = jnp.float32
BF16 = jnp.bfloat16

D_MODEL = 2048
BATCH = 4
SEQ = 2048
DEPTH = 4
DEC_BATCH = 8
DEC_SEQ = 32
PAST_LEN = 4096
CHUNK_SHIFT = 6
GMLP_CHUNK = 128
GMLP_GROUPS = 8
D_A = D_MODEL
GMLP_GDIM = D_A // GMLP_GROUPS
HEAD_DIM = 128
N_HEADS = D_MODEL // HEAD_DIM
N_KV_HEADS = 4
Q_PER_KV = N_HEADS // N_KV_HEADS
D_B = N_HEADS * HEAD_DIM
KV_WIDTH = N_KV_HEADS * HEAD_DIM
IDX_HEADS = 16
IDX_DIM = 128
IDX_WIDTH = IDX_HEADS * IDX_DIM
TOPK_MAX = 256
D_FF = 5632
ALPHA = (2 * DEPTH) ** 0.25
LN_EPS = 1e-5
IDX_SCALE = (IDX_HEADS * IDX_DIM) ** -0.5
ATTN_SCALE = HEAD_DIM ** -0.5
LOG2_E = 1.4426950408889634
IDX_KEY_CHUNK = 512
DSA_BUCKETS = 8
DOWN_ROWS = 256
ADA_WIDTH = 9 * D_MODEL
ADA_ROWS = 16

COL_U = 0
COL_SV = D_A
COL_Q = 2 * D_A
COL_K = COL_Q + D_B
COL_V = COL_K + KV_WIDTH
COL_QI = COL_V + KV_WIDTH
COL_KI = COL_QI + IDX_WIDTH
COL_WI = COL_KI + IDX_DIM
COL_GATES = COL_WI + IDX_HEADS
IN_TN = 1024
IN_MAIN_WIDTH = 10 * IN_TN

LANE = 128
VMEM_LIMIT = 56 * 1024 * 1024
NT_DIMS = (((1,), (1,)), ((), ()))


def _params(*sem):
    return pltpu.CompilerParams(dimension_semantics=sem, vmem_limit_bytes=VMEM_LIMIT)


def _layer_norm_rows(r, g, b):
    mu = jnp.mean(r, axis=-1, keepdims=True)
    rc = r - mu
    var = jnp.mean(rc * rc, axis=-1, keepdims=True)
    return rc * lax.rsqrt(var + LN_EPS) * g + b


def _key_to_float(key):
    return lax.bitcast_convert_type(key ^ ((key >> 31) & 0x7FFFFFFF), F32)


def _kth_largest_key(count_ge, shape, n_sel):
    zero = jnp.zeros(shape, jnp.int32)
    prefix = jnp.where(count_ge(zero) >= n_sel, zero, jnp.int32(-2 ** 31))
    for bit in range(30, -1, -1):
        cand = prefix | jnp.int32(1 << bit)
        prefix = jnp.where(count_ge(cand) >= n_sel, cand, prefix)
    return prefix


def _ada_kernel(c_ref, w_ref, b_ref, o_ref):
    c = c_ref[...]
    s = (c * jax.nn.sigmoid(c)).astype(BF16)
    o_ref[0] = jnp.dot(s, w_ref[0].astype(BF16), preferred_element_type=F32) + b_ref[0]


def _ada(c_all, w_ada, b_ada):
    tn = 1024
    return pl.pallas_call(
        _ada_kernel,
        out_shape=jax.ShapeDtypeStruct((DEPTH, ADA_ROWS, ADA_WIDTH), F32),
        grid=(DEPTH, ADA_WIDTH // tn),
        in_specs=[
            pl.BlockSpec((ADA_ROWS, D_MODEL), lambda l, j: (0, 0)),
            pl.BlockSpec((1, D_MODEL, tn), lambda l, j: (l, 0, j)),
            pl.BlockSpec((1, 1, tn), lambda l, j: (l, 0, j)),
        ],
        out_specs=pl.BlockSpec((1, ADA_ROWS, tn), lambda l, j: (l, 0, j)),
        compiler_params=_params("parallel", "parallel"),
        name="ada",
    )(c_all, w_ada, b_ada.reshape(DEPTH, 1, ADA_WIDTH))


def _mod_kernel(x_ref, sh_ref, sc_ref, h_ref):
    h_ref[...] = (x_ref[...] * (1.0 + sc_ref[0]) + sh_ref[0]).astype(BF16)


def _mod_spec(mod, tiles_per_group):
    return pl.BlockSpec((1, mod.shape[1], D_MODEL), lambda i, *_: (i // tiles_per_group, 0, 0))


def _modulate(x, shift, scale, tm, tpg):
    m = x.shape[0]
    return pl.pallas_call(
        _mod_kernel,
        out_shape=jax.ShapeDtypeStruct((m, D_MODEL), BF16),
        grid=(m // tm,),
        in_specs=[pl.BlockSpec((tm, D_MODEL), lambda i: (i, 0)), _mod_spec(shift, tpg), _mod_spec(scale, tpg)],
        out_specs=pl.BlockSpec((tm, D_MODEL), lambda i: (i, 0)),
        compiler_params=_params("parallel"),
        name="modulate",
    )(x, shift, scale)


def _up_kernel(h_ref, wa_ref, wb_ref, o_ref, wa_s, wb_s):
    @pl.when(pl.program_id(1) == 0)
    def _():
        wa_s[...] = wa_ref[0, 0].astype(BF16)
        wb_s[...] = wb_ref[0, 0].astype(BF16)

    h = h_ref[...]
    a = jnp.dot(h, wa_s[...], preferred_element_type=F32)
    b = jnp.dot(h, wb_s[...], preferred_element_type=F32)
    o_ref[...] = (a * jax.nn.sigmoid(a) * b).astype(BF16)


def _ffn_up(h, w_up, l, s, tm):
    m = h.shape[0]
    tf = 512
    nf = D_FF // tf
    return pl.pallas_call(
        _up_kernel,
        out_shape=jax.ShapeDtypeStruct((m, D_FF), BF16),
        grid=(nf, m // tm),
        in_specs=[
            pl.BlockSpec((tm, D_MODEL), lambda j, i: (i, 0)),
            pl.BlockSpec((1, 1, D_MODEL, tf), lambda j, i: (l, s, 0, j)),
            pl.BlockSpec((1, 1, D_MODEL, tf), lambda j, i: (l, s, 0, j + nf)),
        ],
        out_specs=pl.BlockSpec((tm, tf), lambda j, i: (i, j)),
        scratch_shapes=[pltpu.VMEM((D_MODEL, tf), BF16), pltpu.VMEM((D_MODEL, tf), BF16)],
        compiler_params=_params("parallel", "arbitrary"),
        name="ffn_up",
    )(h, w_up, w_up)


def _down_kernel(*refs, gscale, emit_h):
    if emit_h:
        a_ref, w_ref, x_ref, g_ref, lg_ref, lb_ref, sh_ref, sc_ref, xo_ref, ho_ref = refs
    else:
        a_ref, w_ref, x_ref, g_ref, lg_ref, lb_ref, xo_ref = refs
    y = jnp.dot(a_ref[...], w_ref[0, 0], preferred_element_type=F32)
    r = ALPHA * x_ref[...] + (gscale * g_ref[0]) * y
    xn = _layer_norm_rows(r, lg_ref[...], lb_ref[...])
    xo_ref[...] = xn
    if emit_h:
        ho_ref[...] = (xn * (1.0 + sc_ref[0]) + sh_ref[0]).astype(BF16)


def _proj_resid_ln(a, w, l, s, x, gate, lg, lb, nxt, gscale, tm, rows_per_group):
    m, kdim = a.shape
    tm = min(tm, DOWN_ROWS) if kdim == D_FF else tm
    tpg = max(rows_per_group // tm, 1)
    emit_h = nxt is not None
    row = lambda i: (i, 0)
    const = lambda i: (0, 0)
    in_specs = [
        pl.BlockSpec((tm, kdim), row),
        pl.BlockSpec((1, 1, kdim, D_MODEL), lambda i: (l, s, 0, 0), pipeline_mode=pl.Buffered(1)),
        pl.BlockSpec((tm, D_MODEL), row),
        _mod_spec(gate, tpg),
        pl.BlockSpec((1, D_MODEL), const),
        pl.BlockSpec((1, D_MODEL), const),
    ]
    args = [a, w, x, gate, lg, lb]
    out_shape = [jax.ShapeDtypeStruct((m, D_MODEL), F32)]
    out_specs = [pl.BlockSpec((tm, D_MODEL), row)]
    if emit_h:
        in_specs += [_mod_spec(nxt[0], tpg), _mod_spec(nxt[1], tpg)]
        args += [nxt[0], nxt[1]]
        out_shape.append(jax.ShapeDtypeStruct((m, D_MODEL), BF16))
        out_specs.append(pl.BlockSpec((tm, D_MODEL), row))
    outs = pl.pallas_call(
        functools.partial(_down_kernel, gscale=gscale, emit_h=emit_h),
        out_shape=out_shape,
        grid=(m // tm,),
        in_specs=in_specs,
        out_specs=out_specs,
        compiler_params=_params("arbitrary"),
        name="proj_resid_ln",
    )(*args)
    return (outs[0], outs[1]) if emit_h else (outs[0], None)


def _in_proj_kernel(h_ref, w_ref, o_ref, w_s):
    @pl.when(pl.program_id(1) == 0)
    def _():
        w_s[...] = w_ref[0].T.astype(BF16)

    o_ref[...] = jnp.dot(h_ref[...], w_s[...], preferred_element_type=F32)


def _in_proj(h, w_in_t, l, tm):
    m = h.shape[0]
    return pl.pallas_call(
        _in_proj_kernel,
        out_shape=jax.ShapeDtypeStruct((m, IN_MAIN_WIDTH), F32),
        grid=(IN_MAIN_WIDTH // IN_TN, m // tm),
        in_specs=[pl.BlockSpec((tm, D_MODEL), lambda j, i: (i, 0)),
                  pl.BlockSpec((1, IN_TN, D_MODEL), lambda j, i: (l, j, 0))],
        out_specs=pl.BlockSpec((tm, IN_TN), lambda j, i: (i, j)),
        scratch_shapes=[pltpu.VMEM((D_MODEL, IN_TN), BF16)],
        compiler_params=_params("parallel", "arbitrary"),
        name="in_proj",
    )(h, w_in_t)


def _gmlp_kernel(*refs, tr, chunk, sample):
    (u_ref, sv_ref, k_ref, v_ref, ki_ref, sg_ref, sb_ref, ws_ref, bias_ref, kg_ref, kb_ref,
     oa_ref, k4_ref, v4_ref, kbf_ref, kiln_ref, kibf_ref, vx_ref) = refs[:18]
    u = jax.nn.gelu(u_ref[...])
    svn = _layer_norm_rows(jax.nn.gelu(sv_ref[...]), sg_ref[...], sb_ref[...])
    if sample:
        refs[18][...] = svn
    svb = svn.astype(BF16)
    row = lax.broadcasted_iota(jnp.int32, (tr, tr), 0)
    col = lax.broadcasted_iota(jnp.int32, (tr, tr), 1)
    shift = chunk.bit_length() - 1
    visible = ((row >> shift) == (col >> shift)) & (col <= row)
    for g in range(GMLP_GROUPS):
        cols = slice(g * GMLP_GDIM, (g + 1) * GMLP_GDIM)
        wm = jnp.where(visible, ws_ref[g], 0.0).astype(BF16)
        f = jnp.dot(wm, svb[:, cols], preferred_element_type=F32) + bias_ref[:, cols]
        oa_ref[:, cols] = (u[:, cols] * f).astype(BF16)
    k = k_ref[...]
    v = v_ref[...]
    for h in range(N_KV_HEADS):
        k4_ref[:, h, :] = k[:, h * HEAD_DIM:(h + 1) * HEAD_DIM]
        v4_ref[:, h, :] = v[:, h * HEAD_DIM:(h + 1) * HEAD_DIM]
    kbf_ref[...] = k.astype(BF16)
    vx_ref[...] = v.astype(BF16)
    kiln = _layer_norm_rows(ki_ref[...], kg_ref[...], kb_ref[...])
    kiln_ref[...] = kiln
    kibf_ref[...] = kiln.astype(BF16)


def _gmlp(p, sgu_g, sgu_b, w_tiled, bias_tiled, ki_g, ki_b, chunk, tr, sample):
    m = p.shape[0]
    row = lambda c: (lambda i: (i, c))
    const2 = lambda i: (0, 0)
    out_shape = [
        jax.ShapeDtypeStruct((m, D_A), BF16),
        jax.ShapeDtypeStruct((m, N_KV_HEADS, HEAD_DIM), F32),
        jax.ShapeDtypeStruct((m, N_KV_HEADS, HEAD_DIM), F32),
        jax.ShapeDtypeStruct((m, KV_WIDTH), BF16),
        jax.ShapeDtypeStruct((m, IDX_DIM), F32),
        jax.ShapeDtypeStruct((m, IDX_DIM), BF16),
        jax.ShapeDtypeStruct((m, KV_WIDTH), BF16),
    ]
    head4 = pl.BlockSpec((tr, N_KV_HEADS, HEAD_DIM), lambda i: (i, 0, 0))
    out_specs = [
        pl.BlockSpec((tr, D_A), row(0)), head4, head4,
        pl.BlockSpec((tr, KV_WIDTH), row(0)),
        pl.BlockSpec((tr, IDX_DIM), row(0)),
        pl.BlockSpec((tr, IDX_DIM), row(0)),
        pl.BlockSpec((tr, KV_WIDTH), row(0)),
    ]
    if sample:
        out_shape.append(jax.ShapeDtypeStruct((m, D_A), F32))
        out_specs.append(pl.BlockSpec((tr, D_A), row(0)))
    return pl.pallas_call(
        functools.partial(_gmlp_kernel, tr=tr, chunk=chunk, sample=sample),
        out_shape=out_shape,
        grid=(m // tr,),
        in_specs=[
            pl.BlockSpec((tr, D_A), row(COL_U // D_A)),
            pl.BlockSpec((tr, D_A), row(COL_SV // D_A)),
            pl.BlockSpec((tr, KV_WIDTH), row(COL_K // KV_WIDTH)),
            pl.BlockSpec((tr, KV_WIDTH), row(COL_V // KV_WIDTH)),
            pl.BlockSpec((tr, IDX_DIM), row(COL_KI // IDX_DIM)),
            pl.BlockSpec((1, D_A), const2),
            pl.BlockSpec((1, D_A), const2),
            pl.BlockSpec((GMLP_GROUPS, tr, tr), lambda i: (0, 0, 0)),
            pl.BlockSpec((tr, D_A), const2),
            pl.BlockSpec((1, IDX_DIM), const2),
            pl.BlockSpec((1, IDX_DIM), const2),
        ],
        out_specs=out_specs,
        compiler_params=_params("parallel"),
        name="gmlp",
    )(p, p, p, p, p, sgu_g, sgu_b, w_tiled, bias_tiled, ki_g, ki_b)


def _dsa_prompt_body(i, q_ref, qi0_ref, qi1_ref, wi_ref, k_ref, v_ref, ki_ref, o_ref, score_ref, score_t_ref,
                     *, tq, n_keys, n_sel):
    half = IDX_HEADS // 2
    wi = wi_ref[...]
    qi_rows = jnp.concatenate(
        [(qi0_ref if h < half else qi1_ref)[:, (h % half) * IDX_DIM:(h % half + 1) * IDX_DIM]
         for h in range(IDX_HEADS)], axis=0).astype(BF16)
    for c0 in range(0, n_keys, IDX_KEY_CHUNK):
        c1 = min(c0 + IDX_KEY_CHUNK, n_keys)
        logits = lax.dot_general(qi_rows, ki_ref[c0:c1, :], NT_DIMS, preferred_element_type=F32)
        total = jnp.maximum(logits[0:tq], 0.0) * wi[:, 0:1]
        for h in range(1, IDX_HEADS):
            total = total + jnp.maximum(logits[h * tq:(h + 1) * tq], 0.0) * wi[:, h:h + 1]
        score_ref[:, c0:c1] = total * IDX_SCALE
    qpos = i * tq + lax.broadcasted_iota(jnp.int32, (tq, n_keys), 0)
    kpos = lax.broadcasted_iota(jnp.int32, (tq, n_keys), 1)
    admissible = (kpos >> CHUNK_SHIFT) <= (qpos >> CHUNK_SHIFT)
    score_ref[:, 0:n_keys] = jnp.where(admissible, score_ref[:, 0:n_keys], -jnp.inf)
    n_adm = jnp.sum(jnp.where(admissible, 1.0, 0.0), axis=-1, keepdims=True)

    for c0 in range(0, n_keys, tq):
        score_t_ref[c0:c0 + tq, :] = score_ref[:, c0:c0 + tq].T

    def count_ge(key):
        cf = _key_to_float(key)
        part = jnp.where(score_t_ref[0:tq, :] >= cf, 1.0, 0.0)
        for r0 in range(tq, n_keys, tq):
            part = part + jnp.where(score_t_ref[r0:r0 + tq, :] >= cf, 1.0, 0.0)
        return jnp.sum(part, axis=0, keepdims=True)

    prefix = _kth_largest_key(count_ge, (1, tq), n_sel)
    tau_col = jnp.broadcast_to(_key_to_float(prefix), (tq, tq)).T[:, 0:1]
    tau = jnp.where(n_adm <= n_sel, -jnp.inf, tau_col)
    bias = jnp.where(admissible, jnp.where(score_ref[:, 0:n_keys] >= tau, 0.0, -jnp.inf), -jnp.inf)

    for hk in range(N_KV_HEADS):
        heads = [hk * Q_PER_KV + g for g in range(Q_PER_KV)]
        cols = slice(hk * HEAD_DIM, (hk + 1) * HEAD_DIM)
        qs = jnp.concatenate([q_ref[:, h * HEAD_DIM:(h + 1) * HEAD_DIM] for h in heads], axis=0)
        qs = (qs * (ATTN_SCALE * LOG2_E)).astype(BF16)
        s = lax.dot_general(qs, k_ref[0:n_keys, cols], NT_DIMS, preferred_element_type=F32)
        ps, ls = [], []
        for g in range(Q_PER_KV):
            sg = s[g * tq:(g + 1) * tq] + bias
            pg = jnp.exp2(sg - jnp.max(sg, axis=-1, keepdims=True))
            ls.append(jnp.sum(pg, axis=-1, keepdims=True))
            ps.append(pg.astype(BF16))
        o = jnp.dot(jnp.concatenate(ps, axis=0), v_ref[0:n_keys, cols], preferred_element_type=F32)
        for g, h in enumerate(heads):
            o_ref[:, h * HEAD_DIM:(h + 1) * HEAD_DIM] = (o[g * tq:(g + 1) * tq] / ls[g]).astype(BF16)


def _dsa_prompt_kernel(*refs, tq, nq, n_sel, n_buckets):
    i = pl.program_id(1)
    per = nq // n_buckets
    for c in range(n_buckets):
        @pl.when((i >= c * per) & (i < (c + 1) * per))
        def _(c=c):
            _dsa_prompt_body(i, *refs, tq=tq, n_keys=(c + 1) * per * tq, n_sel=n_sel)


def _dsa_prompt(p, kbf, vbf, kibf, nb, t):
    tq = 128
    nq = t // tq
    n_sel = min(TOPK_MAX, t // 4)
    qrow = lambda c: (lambda b, i: (b * nq + i, c))
    return pl.pallas_call(
        functools.partial(_dsa_prompt_kernel, tq=tq, nq=nq, n_sel=n_sel, n_buckets=DSA_BUCKETS),
        out_shape=jax.ShapeDtypeStruct((nb * t, D_B), BF16),
        grid=(nb, nq),
        in_specs=[
            pl.BlockSpec((tq, D_B), qrow(COL_Q // D_B)),
            pl.BlockSpec((tq, IDX_WIDTH // 2), qrow(2 * COL_QI // IDX_WIDTH)),
            pl.BlockSpec((tq, IDX_WIDTH // 2), qrow(2 * COL_QI // IDX_WIDTH + 1)),
            pl.BlockSpec((tq, LANE), qrow(COL_WI // LANE)),
            pl.BlockSpec((t, KV_WIDTH), lambda b, i: (b, 0)),
            pl.BlockSpec((t, KV_WIDTH), lambda b, i: (b, 0)),
            pl.BlockSpec((t, IDX_DIM), lambda b, i: (b, 0)),
        ],
        out_specs=pl.BlockSpec((tq, D_B), lambda b, i: (b * nq + i, 0)),
        scratch_shapes=[pltpu.VMEM((tq, t), F32), pltpu.VMEM((t, tq), F32)],
        compiler_params=_params("parallel", "parallel"),
        name="dsa_prompt",
    )(p, p, p, p, kbf, vbf, kibf)


def _dsa_sample_kernel(q_ref, qi0_ref, qi1_ref, wi_ref, kn_ref, vn_ref, kin_ref, kp_ref, vp_ref, kip_ref,
                       o_ref, kp_s, vp_s, kip_s, sp_ref, *, tq, n_chunks, chunk_len, n_sel):
    c = pl.program_id(1)
    off = pl.multiple_of(c * chunk_len, chunk_len)
    for h in range(N_KV_HEADS):
        kp_s[h, pl.ds(off, chunk_len), :] = kp_ref[0, 0, pl.ds(h, chunk_len, stride=N_KV_HEADS), :].astype(BF16)
        vp_s[h, pl.ds(off, chunk_len), :] = vp_ref[0, 0, pl.ds(h, chunk_len, stride=N_KV_HEADS), :].astype(BF16)
    kip_s[pl.ds(off, chunk_len), :] = kip_ref[0, 0].astype(BF16)

    @pl.when(c == n_chunks - 1)
    def _():
        half = IDX_HEADS // 2
        wi = wi_ref[...]
        qi_rows = jnp.concatenate(
            [(qi0_ref if h < half else qi1_ref)[:, (h % half) * IDX_DIM:(h % half + 1) * IDX_DIM]
             for h in range(IDX_HEADS)], axis=0).astype(BF16)

        def index_scores(keys):
            logits = lax.dot_general(qi_rows, keys, NT_DIMS, preferred_element_type=F32)
            total = jnp.maximum(logits[0:tq], 0.0) * wi[:, 0:1]
            for h in range(1, IDX_HEADS):
                total = total + jnp.maximum(logits[h * tq:(h + 1) * tq], 0.0) * wi[:, h:h + 1]
            return total * IDX_SCALE

        for j in range(n_chunks):
            sp_ref[:, j * chunk_len:(j + 1) * chunk_len] = index_scores(kip_s[j * chunk_len:(j + 1) * chunk_len, :])
        s_new = index_scores(kin_ref[...])

        def count_ge(key):
            cf = _key_to_float(key)
            return (jnp.sum(jnp.where(sp_ref[...] >= cf, 1.0, 0.0), axis=-1, keepdims=True)
                    + jnp.sum(jnp.where(s_new >= cf, 1.0, 0.0), axis=-1, keepdims=True))

        tau = _key_to_float(_kth_largest_key(count_ge, (tq, 1), n_sel))
        bias_p = jnp.where(sp_ref[...] >= tau, 0.0, -jnp.inf)
        bias_n = jnp.where(s_new >= tau, 0.0, -jnp.inf)

        for hk in range(N_KV_HEADS):
            heads = [hk * Q_PER_KV + g for g in range(Q_PER_KV)]
            cols = slice(hk * HEAD_DIM, (hk + 1) * HEAD_DIM)
            qs = jnp.concatenate([q_ref[:, h * HEAD_DIM:(h + 1) * HEAD_DIM] for h in heads], axis=0).astype(BF16)
            s_p = lax.dot_general(qs, kp_s[hk], NT_DIMS, preferred_element_type=F32) * ATTN_SCALE
            s_n = lax.dot_general(qs, kn_ref[:, cols], NT_DIMS, preferred_element_type=F32) * ATTN_SCALE
            pps, pns, ls = [], [], []
            for g in range(Q_PER_KV):
                rows = slice(g * tq, (g + 1) * tq)
                sp_g = s_p[rows] + bias_p
                sn_g = s_n[rows] + bias_n
                mx = jnp.maximum(jnp.max(sp_g, axis=-1, keepdims=True), jnp.max(sn_g, axis=-1, keepdims=True))
                pp = jnp.exp(sp_g - mx)
                pn = jnp.exp(sn_g - mx)
                ls.append(jnp.sum(pp, axis=-1, keepdims=True) + jnp.sum(pn, axis=-1, keepdims=True))
                pps.append(pp.astype(BF16))
                pns.append(pn.astype(BF16))
            o = (jnp.dot(jnp.concatenate(pps, axis=0), vp_s[hk], preferred_element_type=F32)
                 + jnp.dot(jnp.concatenate(pns, axis=0), vn_ref[:, cols], preferred_element_type=F32))
            for g, h in enumerate(heads):
                o_ref[:, h * HEAD_DIM:(h + 1) * HEAD_DIM] = (o[g * tq:(g + 1) * tq] / ls[g]).astype(BF16)


def _dsa_sample(p, kbf, vbf, kibf, cache_k, cache_v, cache_kidx, l, nb, t):
    assert PAST_LEN % (1 << CHUNK_SHIFT) == 0 and t <= (1 << CHUNK_SHIFT)
    n_chunks = 4
    chunk_len = PAST_LEN // n_chunks
    n_sel = min(TOPK_MAX, (PAST_LEN + t) // 4)
    qrow = lambda c: (lambda b, j: (b, c))
    new = lambda b, j: (b, 0)
    past5 = pl.BlockSpec((1, 1, chunk_len * N_KV_HEADS, HEAD_DIM), lambda b, j: (l, b, j, 0))
    rows_per_batch = PAST_LEN * N_KV_HEADS
    cache_k = cache_k.reshape(DEPTH, nb, rows_per_batch, HEAD_DIM)
    cache_v = cache_v.reshape(DEPTH, nb, rows_per_batch, HEAD_DIM)
    return pl.pallas_call(
        functools.partial(_dsa_sample_kernel, tq=t, n_chunks=n_chunks, chunk_len=chunk_len, n_sel=n_sel),
        out_shape=jax.ShapeDtypeStruct((nb * t, D_B), BF16),
        grid=(nb, n_chunks),
        in_specs=[
            pl.BlockSpec((t, D_B), qrow(COL_Q // D_B)),
            pl.BlockSpec((t, IDX_WIDTH // 2), qrow(2 * COL_QI // IDX_WIDTH)),
            pl.BlockSpec((t, IDX_WIDTH // 2), qrow(2 * COL_QI // IDX_WIDTH + 1)),
            pl.BlockSpec((t, LANE), qrow(COL_WI // LANE)),
            pl.BlockSpec((t, KV_WIDTH), new),
            pl.BlockSpec((t, KV_WIDTH), new),
            pl.BlockSpec((t, IDX_DIM), new),
            past5, past5,
            pl.BlockSpec((1, 1, chunk_len, IDX_DIM), lambda b, j: (l, b, j, 0)),
        ],
        out_specs=pl.BlockSpec((t, D_B), new),
        scratch_shapes=[pltpu.VMEM((N_KV_HEADS, PAST_LEN, HEAD_DIM), BF16),
                        pltpu.VMEM((N_KV_HEADS, PAST_LEN, HEAD_DIM), BF16),
                        pltpu.VMEM((PAST_LEN, IDX_DIM), BF16),
                        pltpu.VMEM((t, PAST_LEN), F32)],
        compiler_params=_params("parallel", "arbitrary"),
        name="dsa_sample",
    )(p, p, p, p, kbf, vbf, kibf, cache_k, cache_v, cache_kidx)


def _merge_kernel(h_ref, a_ref, b_ref, wga_ref, wgb_ref, wa_ref, wb_ref, o_ref):
    h = h_ref[...]
    ga = jax.nn.sigmoid(lax.dot_general(h, wga_ref[0], NT_DIMS, preferred_element_type=F32))
    gb = jax.nn.sigmoid(lax.dot_general(h, wgb_ref[0], NT_DIMS, preferred_element_type=F32))
    ya = jnp.dot(a_ref[...], wa_ref[0], preferred_element_type=F32)
    yb = jnp.dot(b_ref[...], wb_ref[0], preferred_element_type=F32)
    o_ref[...] = (ga * ya + gb * yb).astype(BF16)


def _merge(h, out_a, out_b, w_gates, w_pa, w_pb, l, tm):
    m = h.shape[0]
    tn = 512
    nn = D_MODEL // tn
    rows = pl.BlockSpec((tm, D_MODEL), lambda j, i: (i, 0))
    wspec = lambda off: pl.BlockSpec((1, D_MODEL, tn), lambda j, i: (l, 0, j + off))
    gspec = lambda off: pl.BlockSpec((1, tn, D_MODEL), lambda j, i: (l, j + off, 0))
    return pl.pallas_call(
        _merge_kernel,
        out_shape=jax.ShapeDtypeStruct((m, D_MODEL), BF16),
        grid=(nn, m // tm),
        in_specs=[rows, rows, rows, gspec(0), gspec(nn), wspec(0), wspec(0)],
        out_specs=pl.BlockSpec((tm, tn), lambda j, i: (i, j)),
        compiler_params=_params("parallel", "parallel"),
        name="merge",
    )(h, out_a, out_b, w_gates, w_gates, w_pa, w_pb)


def _run_trunk(x, mods, wts, nb, t, tm, tr, chunk, caches):
    sample = caches is not None
    tm_wide = min(2 * tm, x.shape[0])
    tpg = max(t // tm, 1)
    h = _modulate(x, mods(0, 0, 0), mods(0, 0, 1), tm, tpg)
    new_k, new_v, new_ki, new_sv = [], [], [], []
    for l in range(DEPTH):
        ln = lambda s: (wts["ln_g"][l, s][None], wts["ln_b"][l, s][None])
        act = _ffn_up(h, wts["up"], l, 0, tm_wide)
        x, h = _proj_resid_ln(act, wts["down"], l, 0, x, mods(l, 0, 2), *ln(0),
                              (mods(l, 1, 0), mods(l, 1, 1)), 0.5, tm, t)
        p = _in_proj(h, wts["in"], l, tm_wide)
        outs = _gmlp(p, wts["sgu_g"][l][None], wts["sgu_b"][l][None], *wts["spatial"][l][chunk],
                     wts["ki_g"][l][None], wts["ki_b"][l][None], chunk, tr, sample)
        out_a, k4, v4, kbf, kiln, kibf, vx = outs[:7]
        if sample:
            out_b = _dsa_sample(p, kbf, vx, kibf, *caches, l, nb, t)
            new_sv.append(outs[7].reshape(nb, t, D_A))
        else:
            out_b = _dsa_prompt(p, kbf, vx, kibf, nb, t)
        z = _merge(h, out_a, out_b, wts["gates"], wts["pa"], wts["pb"], l, tm)
        x, h = _proj_resid_ln(z, wts["out"], l, 0, x, mods(l, 1, 2), *ln(1),
                              (mods(l, 2, 0), mods(l, 2, 1)), 1.0, tm, t)
        act = _ffn_up(h, wts["up"], l, 1, tm_wide)
        nxt = (mods(l + 1, 0, 0), mods(l + 1, 0, 1)) if l + 1 < DEPTH else None
        x, h = _proj_resid_ln(act, wts["down"], l, 1, x, mods(l, 2, 2), *ln(2), nxt, 0.5, tm, t)
        new_k.append(k4.reshape(nb, t, N_KV_HEADS, HEAD_DIM))
        new_v.append(v4.reshape(nb, t, N_KV_HEADS, HEAD_DIM))
        new_ki.append(kiln.reshape(nb, t, IDX_DIM))
    return x, jnp.stack(new_k), jnp.stack(new_v), jnp.stack(new_ki), (jnp.stack(new_sv) if sample else None)


def kernel(x_prompt, x_sample, c_prompt, c_sample, cache_k, cache_v, cache_kidx, w_ada, b_ada, w_ffn_up,
           w_ffn_down, w_in, sgu_ln_g, sgu_ln_b, w_spatial, b_spatial, kidx_ln_g, kidx_ln_b, w_branch_a,
           w_branch_b, w_out, ln_g, ln_b):
    tr = 256
    c_all = jnp.concatenate([c_prompt, c_sample, jnp.zeros((ADA_ROWS - BATCH - DEC_BATCH, D_MODEL), F32)], axis=0)
    ada = _ada(c_all, w_ada, b_ada)

    def mod_cols(l, sub, kind, rows):
        c0 = (sub * 3 + kind) * D_MODEL
        return ada[l, rows, c0:c0 + D_MODEL]

    def mods_prompt(l, sub, kind):
        return mod_cols(l, sub, kind, slice(0, BATCH))[:, None, :]

    def mods_sample(l, sub, kind):
        rows = mod_cols(l, sub, kind, slice(BATCH, BATCH + DEC_BATCH))
        return jnp.repeat(rows, DEC_SEQ, axis=0)[None]

    def spatial(l, chunk):
        reps = tr // chunk
        w = jnp.tile(w_spatial[l][:, :chunk, :chunk], (1, reps, reps))
        b = jnp.tile(jnp.repeat(jnp.transpose(b_spatial[l][:, :chunk]), GMLP_GDIM, axis=1), (reps, 1))
        return w, b

    w_in_t = jnp.swapaxes(w_in, 1, 2)
    wts = dict(
        up=w_ffn_up, **{"in": w_in_t},
        down=w_ffn_down.astype(BF16),
        out=w_out.astype(BF16)[:, None],
        pa=w_branch_a.astype(BF16), pb=w_branch_b.astype(BF16),
        gates=w_in_t[:, COL_GATES:COL_GATES + 2 * D_MODEL, :].astype(BF16),
        ln_g=ln_g, ln_b=ln_b, sgu_g=sgu_ln_g, sgu_b=sgu_ln_b, ki_g=kidx_ln_g, ki_b=kidx_ln_b,
        spatial=[{c: spatial(l, c) for c in (GMLP_CHUNK, DEC_SEQ)} for l in range(DEPTH)],
    )

    y_p, k_p, v_p, ki_p, _ = _run_trunk(x_prompt.reshape(BATCH * SEQ, D_MODEL), mods_prompt, wts,
                                        BATCH, SEQ, 512, tr, GMLP_CHUNK, None)
    m_s = DEC_BATCH * DEC_SEQ
    y_s, k_s, v_s, ki_s, sv_s = _run_trunk(x_sample.reshape(m_s, D_MODEL), mods_sample, wts,
                                           DEC_BATCH, DEC_SEQ, m_s, tr, DEC_SEQ, (cache_k, cache_v, cache_kidx))
    return (y_p.reshape(BATCH, SEQ, D_MODEL), y_s.reshape(DEC_BATCH, DEC_SEQ, D_MODEL),
            k_p, v_p, ki_p, k_s, v_s, ki_s, sv_s)
```

```python
import functools

import jax
import jax.numpy as jnp
from jax import lax
from jax.experimental import pallas as pl
from jax.experimental.pallas import tpu as pltpu

F32 = jnp.float32
BF16 = jnp.bfloat16

D_MODEL = 2048
BATCH = 4
SEQ = 2048
DEPTH = 4
DEC_BATCH = 8
DEC_SEQ = 32
PAST_LEN = 4096
CHUNK_SHIFT = 6
GMLP_CHUNK = 128
GMLP_GROUPS = 8
D_A = D_MODEL
GMLP_GDIM = D_A // GMLP_GROUPS
HEAD_DIM = 128
N_HEADS = D_MODEL // HEAD_DIM
N_KV_HEADS = 4
Q_PER_KV = N_HEADS // N_KV_HEADS
D_B = N_HEADS * HEAD_DIM
KV_WIDTH = N_KV_HEADS * HEAD_DIM
IDX_HEADS = 16
IDX_DIM = 128
IDX_WIDTH = IDX_HEADS * IDX_DIM
TOPK_MAX = 256
D_FF = 5632
ALPHA = (2 * DEPTH) ** 0.25
LN_EPS = 1e-5
IDX_SCALE = (IDX_HEADS * IDX_DIM) ** -0.5
ATTN_SCALE = HEAD_DIM ** -0.5
LOG2_E = 1.4426950408889634
IDX_KEY_CHUNK = 512
DSA_BUCKETS = 4
DOWN_ROWS = 256
ADA_WIDTH = 9 * D_MODEL
ADA_ROWS = 16

COL_U = 0
COL_SV = D_A
COL_Q = 2 * D_A
COL_K = COL_Q + D_B
COL_V = COL_K + KV_WIDTH
COL_QI = COL_V + KV_WIDTH
COL_KI = COL_QI + IDX_WIDTH
COL_WI = COL_KI + IDX_DIM
COL_GATES = COL_WI + IDX_HEADS
IN_TN = 1024
IN_MAIN_WIDTH = 10 * IN_TN

LANE = 128
VMEM_LIMIT = 56 * 1024 * 1024
NT_DIMS = (((1,), (1,)), ((), ()))


def _params(*sem):
    return pltpu.CompilerParams(dimension_semantics=sem, vmem_limit_bytes=VMEM_LIMIT)


def _layer_norm_rows(r, g, b):
    mu = jnp.mean(r, axis=-1, keepdims=True)
    rc = r - mu
    var = jnp.mean(rc * rc, axis=-1, keepdims=True)
    return rc * lax.rsqrt(var + LN_EPS) * g + b


def _key_to_float(key):
    return lax.bitcast_convert_type(key ^ ((key >> 31) & 0x7FFFFFFF), F32)


def _kth_largest_key(count_ge, shape, n_sel):
    zero = jnp.zeros(shape, jnp.int32)
    prefix = jnp.where(count_ge(zero) >= n_sel, zero, jnp.int32(-2 ** 31))

    def refine(step, prefix):
        cand = prefix | (jnp.int32(1 << 30) >> step)
        return jnp.where(count_ge(cand) >= n_sel, cand, prefix)

    return lax.fori_loop(0, 31, refine, prefix)


def _ada_kernel(c_ref, w_ref, b_ref, o_ref):
    c = c_ref[...]
    s = (c * jax.nn.sigmoid(c)).astype(BF16)
    o_ref[0] = jnp.dot(s, w_ref[0].astype(BF16), preferred_element_type=F32) + b_ref[0]


def _ada(c_all, w_ada, b_ada):
    tn = 1024
    return pl.pallas_call(
        _ada_kernel,
        out_shape=jax.ShapeDtypeStruct((DEPTH, ADA_ROWS, ADA_WIDTH), F32),
        grid=(DEPTH, ADA_WIDTH // tn),
        in_specs=[
            pl.BlockSpec((ADA_ROWS, D_MODEL), lambda l, j: (0, 0)),
            pl.BlockSpec((1, D_MODEL, tn), lambda l, j: (l, 0, j)),
            pl.BlockSpec((1, 1, tn), lambda l, j: (l, 0, j)),
        ],
        out_specs=pl.BlockSpec((1, ADA_ROWS, tn), lambda l, j: (l, 0, j)),
        compiler_params=_params("parallel", "parallel"),
        name="ada",
    )(c_all, w_ada, b_ada.reshape(DEPTH, 1, ADA_WIDTH))


def _mod_kernel(x_ref, sh_ref, sc_ref, h_ref):
    h_ref[...] = (x_ref[...] * (1.0 + sc_ref[0]) + sh_ref[0]).astype(BF16)


def _mod_spec(mod, tiles_per_group):
    return pl.BlockSpec((1, mod.shape[1], D_MODEL), lambda i, *_: (i // tiles_per_group, 0, 0))


def _modulate(x, shift, scale, tm, tpg):
    m = x.shape[0]
    return pl.pallas_call(
        _mod_kernel,
        out_shape=jax.ShapeDtypeStruct((m, D_MODEL), BF16),
        grid=(m // tm,),
        in_specs=[pl.BlockSpec((tm, D_MODEL), lambda i: (i, 0)), _mod_spec(shift, tpg), _mod_spec(scale, tpg)],
        out_specs=pl.BlockSpec((tm, D_MODEL), lambda i: (i, 0)),
        compiler_params=_params("parallel"),
        name="modulate",
    )(x, shift, scale)


def _up_kernel(h_ref, wa_ref, wb_ref, o_ref, wa_s, wb_s):
    @pl.when(pl.program_id(1) == 0)
    def _():
        wa_s[...] = wa_ref[0, 0].astype(BF16)
        wb_s[...] = wb_ref[0, 0].astype(BF16)

    h = h_ref[...]
    a = jnp.dot(h, wa_s[...], preferred_element_type=F32)
    b = jnp.dot(h, wb_s[...], preferred_element_type=F32)
    o_ref[...] = (a * jax.nn.sigmoid(a) * b).astype(BF16)


def _ffn_up(h, w_up, l, s, tm):
    m = h.shape[0]
    tf = 512
    nf = D_FF // tf
    return pl.pallas_call(
        _up_kernel,
        out_shape=jax.ShapeDtypeStruct((m, D_FF), BF16),
        grid=(nf, m // tm),
        in_specs=[
            pl.BlockSpec((tm, D_MODEL), lambda j, i: (i, 0)),
            pl.BlockSpec((1, 1, D_MODEL, tf), lambda j, i: (l, s, 0, j)),
            pl.BlockSpec((1, 1, D_MODEL, tf), lambda j, i: (l, s, 0, j + nf)),
        ],
        out_specs=pl.BlockSpec((tm, tf), lambda j, i: (i, j)),
        scratch_shapes=[pltpu.VMEM((D_MODEL, tf), BF16), pltpu.VMEM((D_MODEL, tf), BF16)],
        compiler_params=_params("parallel", "arbitrary"),
        name="ffn_up",
    )(h, w_up, w_up)


def _down_kernel(*refs, gscale, emit_h):
    if emit_h:
        a_ref, w_ref, x_ref, g_ref, lg_ref, lb_ref, sh_ref, sc_ref, xo_ref, ho_ref = refs
    else:
        a_ref, w_ref, x_ref, g_ref, lg_ref, lb_ref, xo_ref = refs
    y = jnp.dot(a_ref[...], w_ref[0, 0], preferred_element_type=F32)
    r = ALPHA * x_ref[...] + (gscale * g_ref[0]) * y
    xn = _layer_norm_rows(r, lg_ref[...], lb_ref[...])
    xo_ref[...] = xn
    if emit_h:
        ho_ref[...] = (xn * (1.0 + sc_ref[0]) + sh_ref[0]).astype(BF16)


def _proj_resid_ln(a, w, l, s, x, gate, lg, lb, nxt, gscale, tm, rows_per_group):
    m, kdim = a.shape
    tm = min(tm, DOWN_ROWS) if kdim == D_FF else tm
    tpg = max(rows_per_group // tm, 1)
    emit_h = nxt is not None
    row = lambda i: (i, 0)
    const = lambda i: (0, 0)
    in_specs = [
        pl.BlockSpec((tm, kdim), row),
        pl.BlockSpec((1, 1, kdim, D_MODEL), lambda i: (l, s, 0, 0), pipeline_mode=pl.Buffered(1)),
        pl.BlockSpec((tm, D_MODEL), row),
        _mod_spec(gate, tpg),
        pl.BlockSpec((1, D_MODEL), const),
        pl.BlockSpec((1, D_MODEL), const),
    ]
    args = [a, w, x, gate, lg, lb]
    out_shape = [jax.ShapeDtypeStruct((m, D_MODEL), F32)]
    out_specs = [pl.BlockSpec((tm, D_MODEL), row)]
    if emit_h:
        in_specs += [_mod_spec(nxt[0], tpg), _mod_spec(nxt[1], tpg)]
        args += [nxt[0], nxt[1]]
        out_shape.append(jax.ShapeDtypeStruct((m, D_MODEL), BF16))
        out_specs.append(pl.BlockSpec((tm, D_MODEL), row))
    outs = pl.pallas_call(
        functools.partial(_down_kernel, gscale=gscale, emit_h=emit_h),
        out_shape=out_shape,
        grid=(m // tm,),
        in_specs=in_specs,
        out_specs=out_specs,
        compiler_params=_params("arbitrary"),
        name="proj_resid_ln",
    )(*args)
    return (outs[0], outs[1]) if emit_h else (outs[0], None)


def _in_proj_kernel(h_ref, w_ref, o_ref, w_s):
    @pl.when(pl.program_id(1) == 0)
    def _():
        w_s[...] = w_ref[0].T.astype(BF16)

    o_ref[...] = jnp.dot(h_ref[...], w_s[...], preferred_element_type=F32)


def _in_proj(h, w_in_t, l, tm):
    m = h.shape[0]
    return pl.pallas_call(
        _in_proj_kernel,
        out_shape=jax.ShapeDtypeStruct((m, IN_MAIN_WIDTH), F32),
        grid=(IN_MAIN_WIDTH // IN_TN, m // tm),
        in_specs=[pl.BlockSpec((tm, D_MODEL), lambda j, i: (i, 0)),
                  pl.BlockSpec((1, IN_TN, D_MODEL), lambda j, i: (l, j, 0))],
        out_specs=pl.BlockSpec((tm, IN_TN), lambda j, i: (i, j)),
        scratch_shapes=[pltpu.VMEM((D_MODEL, IN_TN), BF16)],
        compiler_params=_params("parallel", "arbitrary"),
        name="in_proj",
    )(h, w_in_t)


def _gmlp_kernel(*refs, tr, chunk, sample):
    (u_ref, sv_ref, k_ref, v_ref, ki_ref, sg_ref, sb_ref, ws_ref, bias_ref, kg_ref, kb_ref,
     oa_ref, k4_ref, v4_ref, kbf_ref, kiln_ref, kibf_ref, vx_ref) = refs[:18]
    u = jax.nn.gelu(u_ref[...])
    svn = _layer_norm_rows(jax.nn.gelu(sv_ref[...]), sg_ref[...], sb_ref[...])
    if sample:
        refs[18][...] = svn
    svb = svn.astype(BF16)
    row = lax.broadcasted_iota(jnp.int32, (tr, tr), 0)
    col = lax.broadcasted_iota(jnp.int32, (tr, tr), 1)
    shift = chunk.bit_length() - 1
    visible = ((row >> shift) == (col >> shift)) & (col <= row)
    for g in range(GMLP_GROUPS):
        cols = slice(g * GMLP_GDIM, (g + 1) * GMLP_GDIM)
        wm = jnp.where(visible, ws_ref[g], 0.0).astype(BF16)
        f = jnp.dot(wm, svb[:, cols], preferred_element_type=F32) + bias_ref[:, cols]
        oa_ref[:, cols] = (u[:, cols] * f).astype(BF16)
    k = k_ref[...]
    v = v_ref[...]
    for h in range(N_KV_HEADS):
        k4_ref[:, h, :] = k[:, h * HEAD_DIM:(h + 1) * HEAD_DIM]
        v4_ref[:, h, :] = v[:, h * HEAD_DIM:(h + 1) * HEAD_DIM]
    kbf_ref[...] = k.astype(BF16)
    vx_ref[...] = v.astype(BF16)
    kiln = _layer_norm_rows(ki_ref[...], kg_ref[...], kb_ref[...])
    kiln_ref[...] = kiln
    kibf_ref[...] = kiln.astype(BF16)


def _gmlp(p, sgu_g, sgu_b, w_tiled, bias_tiled, ki_g, ki_b, chunk, tr, sample):
    m = p.shape[0]
    row = lambda c: (lambda i: (i, c))
    const2 = lambda i: (0, 0)
    out_shape = [
        jax.ShapeDtypeStruct((m, D_A), BF16),
        jax.ShapeDtypeStruct((m, N_KV_HEADS, HEAD_DIM), F32),
        jax.ShapeDtypeStruct((m, N_KV_HEADS, HEAD_DIM), F32),
        jax.ShapeDtypeStruct((m, KV_WIDTH), BF16),
        jax.ShapeDtypeStruct((m, IDX_DIM), F32),
        jax.ShapeDtypeStruct((m, IDX_DIM), BF16),
        jax.ShapeDtypeStruct((m, KV_WIDTH), BF16),
    ]
    head4 = pl.BlockSpec((tr, N_KV_HEADS, HEAD_DIM), lambda i: (i, 0, 0))
    out_specs = [
        pl.BlockSpec((tr, D_A), row(0)), head4, head4,
        pl.BlockSpec((tr, KV_WIDTH), row(0)),
        pl.BlockSpec((tr, IDX_DIM), row(0)),
        pl.BlockSpec((tr, IDX_DIM), row(0)),
        pl.BlockSpec((tr, KV_WIDTH), row(0)),
    ]
    if sample:
        out_shape.append(jax.ShapeDtypeStruct((m, D_A), F32))
        out_specs.append(pl.BlockSpec((tr, D_A), row(0)))
    return pl.pallas_call(
        functools.partial(_gmlp_kernel, tr=tr, chunk=chunk, sample=sample),
        out_shape=out_shape,
        grid=(m // tr,),
        in_specs=[
            pl.BlockSpec((tr, D_A), row(COL_U // D_A)),
            pl.BlockSpec((tr, D_A), row(COL_SV // D_A)),
            pl.BlockSpec((tr, KV_WIDTH), row(COL_K // KV_WIDTH)),
            pl.BlockSpec((tr, KV_WIDTH), row(COL_V // KV_WIDTH)),
            pl.BlockSpec((tr, IDX_DIM), row(COL_KI // IDX_DIM)),
            pl.BlockSpec((1, D_A), const2),
            pl.BlockSpec((1, D_A), const2),
            pl.BlockSpec((GMLP_GROUPS, tr, tr), lambda i: (0, 0, 0)),
            pl.BlockSpec((tr, D_A), const2),
            pl.BlockSpec((1, IDX_DIM), const2),
            pl.BlockSpec((1, IDX_DIM), const2),
        ],
        out_specs=out_specs,
        compiler_params=_params("parallel"),
        name="gmlp",
    )(p, p, p, p, p, sgu_g, sgu_b, w_tiled, bias_tiled, ki_g, ki_b)


def _dsa_prompt_body(i, q_ref, qi0_ref, qi1_ref, wi_ref, k_ref, v_ref, ki_ref, o_ref, score_ref, score_t_ref,
                     *, tq, n_keys, n_sel):
    half = IDX_HEADS // 2
    wi = wi_ref[...]
    qi_rows = jnp.concatenate(
        [(qi0_ref if h < half else qi1_ref)[:, (h % half) * IDX_DIM:(h % half + 1) * IDX_DIM]
         for h in range(IDX_HEADS)], axis=0).astype(BF16)
    for c0 in range(0, n_keys, IDX_KEY_CHUNK):
        c1 = min(c0 + IDX_KEY_CHUNK, n_keys)
        logits = lax.dot_general(qi_rows, ki_ref[c0:c1, :], NT_DIMS, preferred_element_type=F32)
        total = jnp.maximum(logits[0:tq], 0.0) * wi[:, 0:1]
        for h in range(1, IDX_HEADS):
            total = total + jnp.maximum(logits[h * tq:(h + 1) * tq], 0.0) * wi[:, h:h + 1]
        score_ref[:, c0:c1] = total * IDX_SCALE
    qpos = i * tq + lax.broadcasted_iota(jnp.int32, (tq, n_keys), 0)
    kpos = lax.broadcasted_iota(jnp.int32, (tq, n_keys), 1)
    admissible = (kpos >> CHUNK_SHIFT) <= (qpos >> CHUNK_SHIFT)
    score_ref[:, 0:n_keys] = jnp.where(admissible, score_ref[:, 0:n_keys], -jnp.inf)
    n_adm = jnp.sum(jnp.where(admissible, 1.0, 0.0), axis=-1, keepdims=True)

    for c0 in range(0, n_keys, tq):
        score_t_ref[c0:c0 + tq, :] = score_ref[:, c0:c0 + tq].T

    def count_ge(key):
        cf = _key_to_float(key)
        part = jnp.where(score_t_ref[0:tq, :] >= cf, 1.0, 0.0)
        for r0 in range(tq, n_keys, tq):
            part = part + jnp.where(score_t_ref[r0:r0 + tq, :] >= cf, 1.0, 0.0)
        return jnp.sum(part, axis=0, keepdims=True)

    prefix = _kth_largest_key(count_ge, (1, tq), n_sel)
    tau_col = jnp.broadcast_to(_key_to_float(prefix), (tq, tq)).T[:, 0:1]
    tau = jnp.where(n_adm <= n_sel, -jnp.inf, tau_col)
    bias = jnp.where(admissible, jnp.where(score_ref[:, 0:n_keys] >= tau, 0.0, -jnp.inf), -jnp.inf)

    for hk in range(N_KV_HEADS):
        heads = [hk * Q_PER_KV + g for g in range(Q_PER_KV)]
        cols = slice(hk * HEAD_DIM, (hk + 1) * HEAD_DIM)
        qs = jnp.concatenate([q_ref[:, h * HEAD_DIM:(h + 1) * HEAD_DIM] for h in heads], axis=0)
        qs = (qs * (ATTN_SCALE * LOG2_E)).astype(BF16)
        s = lax.dot_general(qs, k_ref[0:n_keys, cols], NT_DIMS, preferred_element_type=F32)
        ps, ls = [], []
        for g in range(Q_PER_KV):
            sg = s[g * tq:(g + 1) * tq] + bias
            pg = jnp.exp2(sg - jnp.max(sg, axis=-1, keepdims=True))
            ls.append(jnp.sum(pg, axis=-1, keepdims=True))
            ps.append(pg.astype(BF16))
        o = jnp.dot(jnp.concatenate(ps, axis=0), v_ref[0:n_keys, cols], preferred_element_type=F32)
        for g, h in enumerate(heads):
            o_ref[:, h * HEAD_DIM:(h + 1) * HEAD_DIM] = (o[g * tq:(g + 1) * tq] / ls[g]).astype(BF16)


def _dsa_prompt_kernel(*refs, tq, nq, n_sel, n_buckets):
    i = pl.program_id(1)
    per = nq // n_buckets
    for c in range(n_buckets):
        @pl.when((i >= c * per) & (i < (c + 1) * per))
        def _(c=c):
            _dsa_prompt_body(i, *refs, tq=tq, n_keys=(c + 1) * per * tq, n_sel=n_sel)


def _dsa_prompt(p, kbf, vbf, kibf, nb, t):
    tq = 128
    nq = t // tq
    n_sel = min(TOPK_MAX, t // 4)
    qrow = lambda c: (lambda b, i: (b * nq + i, c))
    return pl.pallas_call(
        functools.partial(_dsa_prompt_kernel, tq=tq, nq=nq, n_sel=n_sel, n_buckets=DSA_BUCKETS),
        out_shape=jax.ShapeDtypeStruct((nb * t, D_B), BF16),
        grid=(nb, nq),
        in_specs=[
            pl.BlockSpec((tq, D_B), qrow(COL_Q // D_B)),
            pl.BlockSpec((tq, IDX_WIDTH // 2), qrow(2 * COL_QI // IDX_WIDTH)),
            pl.BlockSpec((tq, IDX_WIDTH // 2), qrow(2 * COL_QI // IDX_WIDTH + 1)),
            pl.BlockSpec((tq, LANE), qrow(COL_WI // LANE)),
            pl.BlockSpec((t, KV_WIDTH), lambda b, i: (b, 0)),
            pl.BlockSpec((t, KV_WIDTH), lambda b, i: (b, 0)),
            pl.BlockSpec((t, IDX_DIM), lambda b, i: (b, 0)),
        ],
        out_specs=pl.BlockSpec((tq, D_B), lambda b, i: (b * nq + i, 0)),
        scratch_shapes=[pltpu.VMEM((tq, t), F32), pltpu.VMEM((t, tq), F32)],
        compiler_params=_params("parallel", "parallel"),
        name="dsa_prompt",
    )(p, p, p, p, kbf, vbf, kibf)


def _dsa_sample_kernel(q_ref, qi0_ref, qi1_ref, wi_ref, kn_ref, vn_ref, kin_ref, kp_ref, vp_ref, kip_ref,
                       o_ref, kp_s, vp_s, kip_s, sp_ref, *, tq, n_chunks, chunk_len, n_sel):
    c = pl.program_id(1)
    off = pl.multiple_of(c * chunk_len, chunk_len)
    for h in range(N_KV_HEADS):
        kp_s[h, pl.ds(off, chunk_len), :] = kp_ref[0, 0, pl.ds(h, chunk_len, stride=N_KV_HEADS), :].astype(BF16)
        vp_s[h, pl.ds(off, chunk_len), :] = vp_ref[0, 0, pl.ds(h, chunk_len, stride=N_KV_HEADS), :].astype(BF16)
    kip_s[pl.ds(off, chunk_len), :] = kip_ref[0, 0].astype(BF16)

    @pl.when(c == n_chunks - 1)
    def _():
        half = IDX_HEADS // 2
        wi = wi_ref[...]
        qi_rows = jnp.concatenate(
            [(qi0_ref if h < half else qi1_ref)[:, (h % half) * IDX_DIM:(h % half + 1) * IDX_DIM]
             for h in range(IDX_HEADS)], axis=0).astype(BF16)

        def index_scores(keys):
            logits = lax.dot_general(qi_rows, keys, NT_DIMS, preferred_element_type=F32)
            total = jnp.maximum(logits[0:tq], 0.0) * wi[:, 0:1]
            for h in range(1, IDX_HEADS):
                total = total + jnp.maximum(logits[h * tq:(h + 1) * tq], 0.0) * wi[:, h:h + 1]
            return total * IDX_SCALE

        for j in range(n_chunks):
            sp_ref[:, j * chunk_len:(j + 1) * chunk_len] = index_scores(kip_s[j * chunk_len:(j + 1) * chunk_len, :])
        s_new = index_scores(kin_ref[...])

        def count_ge(key):
            cf = _key_to_float(key)
            return (jnp.sum(jnp.where(sp_ref[...] >= cf, 1.0, 0.0), axis=-1, keepdims=True)
                    + jnp.sum(jnp.where(s_new >= cf, 1.0, 0.0), axis=-1, keepdims=True))

        tau = _key_to_float(_kth_largest_key(count_ge, (tq, 1), n_sel))
        bias_p = jnp.where(sp_ref[...] >= tau, 0.0, -jnp.inf)
        bias_n = jnp.where(s_new >= tau, 0.0, -jnp.inf)

        for hk in range(N_KV_HEADS):
            heads = [hk * Q_PER_KV + g for g in range(Q_PER_KV)]
            cols = slice(hk * HEAD_DIM, (hk + 1) * HEAD_DIM)
            qs = jnp.concatenate([q_ref[:, h * HEAD_DIM:(h + 1) * HEAD_DIM] for h in heads], axis=0).astype(BF16)
            s_p = lax.dot_general(qs, kp_s[hk], NT_DIMS, preferred_element_type=F32) * ATTN_SCALE
            s_n = lax.dot_general(qs, kn_ref[:, cols], NT_DIMS, preferred_element_type=F32) * ATTN_SCALE
            pps, pns, ls = [], [], []
            for g in range(Q_PER_KV):
                rows = slice(g * tq, (g + 1) * tq)
                sp_g = s_p[rows] + bias_p
                sn_g = s_n[rows] + bias_n
                mx = jnp.maximum(jnp.max(sp_g, axis=-1, keepdims=True), jnp.max(sn_g, axis=-1, keepdims=True))
                pp = jnp.exp(sp_g - mx)
                pn = jnp.exp(sn_g - mx)
                ls.append(jnp.sum(pp, axis=-1, keepdims=True) + jnp.sum(pn, axis=-1, keepdims=True))
                pps.append(pp.astype(BF16))
                pns.append(pn.astype(BF16))
            o = (jnp.dot(jnp.concatenate(pps, axis=0), vp_s[hk], preferred_element_type=F32)
                 + jnp.dot(jnp.concatenate(pns, axis=0), vn_ref[:, cols], preferred_element_type=F32))
            for g, h in enumerate(heads):
                o_ref[:, h * HEAD_DIM:(h + 1) * HEAD_DIM] = (o[g * tq:(g + 1) * tq] / ls[g]).astype(BF16)


def _dsa_sample(p, kbf, vbf, kibf, cache_k, cache_v, cache_kidx, l, nb, t):
    assert PAST_LEN % (1 << CHUNK_SHIFT) == 0 and t <= (1 << CHUNK_SHIFT)
    n_chunks = 4
    chunk_len = PAST_LEN // n_chunks
    n_sel = min(TOPK_MAX, (PAST_LEN + t) // 4)
    qrow = lambda c: (lambda b, j: (b, c))
    new = lambda b, j: (b, 0)
    past5 = pl.BlockSpec((1, 1, chunk_len * N_KV_HEADS, HEAD_DIM), lambda b, j: (l, b, j, 0))
    rows_per_batch = PAST_LEN * N_KV_HEADS
    cache_k = cache_k.reshape(DEPTH, nb, rows_per_batch, HEAD_DIM)
    cache_v = cache_v.reshape(DEPTH, nb, rows_per_batch, HEAD_DIM)
    return pl.pallas_call(
        functools.partial(_dsa_sample_kernel, tq=t, n_chunks=n_chunks, chunk_len=chunk_len, n_sel=n_sel),
        out_shape=jax.ShapeDtypeStruct((nb * t, D_B), BF16),
        grid=(nb, n_chunks),
        in_specs=[
            pl.BlockSpec((t, D_B), qrow(COL_Q // D_B)),
            pl.BlockSpec((t, IDX_WIDTH // 2), qrow(2 * COL_QI // IDX_WIDTH)),
            pl.BlockSpec((t, IDX_WIDTH // 2), qrow(2 * COL_QI // IDX_WIDTH + 1)),
            pl.BlockSpec((t, LANE), qrow(COL_WI // LANE)),
            pl.BlockSpec((t, KV_WIDTH), new),
            pl.BlockSpec((t, KV_WIDTH), new),
            pl.BlockSpec((t, IDX_DIM), new),
            past5, past5,
            pl.BlockSpec((1, 1, chunk_len, IDX_DIM), lambda b, j: (l, b, j, 0)),
        ],
        out_specs=pl.BlockSpec((t, D_B), new),
        scratch_shapes=[pltpu.VMEM((N_KV_HEADS, PAST_LEN, HEAD_DIM), BF16),
                        pltpu.VMEM((N_KV_HEADS, PAST_LEN, HEAD_DIM), BF16),
                        pltpu.VMEM((PAST_LEN, IDX_DIM), BF16),
                        pltpu.VMEM((t, PAST_LEN), F32)],
        compiler_params=_params("parallel", "arbitrary"),
        name="dsa_sample",
    )(p, p, p, p, kbf, vbf, kibf, cache_k, cache_v, cache_kidx)


def _merge_kernel(h_ref, a_ref, b_ref, wga_ref, wgb_ref, wa_ref, wb_ref, o_ref):
    h = h_ref[...]
    ga = jax.nn.sigmoid(lax.dot_general(h, wga_ref[0], NT_DIMS, preferred_element_type=F32))
    gb = jax.nn.sigmoid(lax.dot_general(h, wgb_ref[0], NT_DIMS, preferred_element_type=F32))
    ya = jnp.dot(a_ref[...], wa_ref[0], preferred_element_type=F32)
    yb = jnp.dot(b_ref[...], wb_ref[0], preferred_element_type=F32)
    o_ref[...] = (ga * ya + gb * yb).astype(BF16)


def _merge(h, out_a, out_b, w_gates, w_pa, w_pb, l, tm):
    m = h.shape[0]
    tn = 512
    nn = D_MODEL // tn
    rows = pl.BlockSpec((tm, D_MODEL), lambda j, i: (i, 0))
    wspec = lambda off: pl.BlockSpec((1, D_MODEL, tn), lambda j, i: (l, 0, j + off))
    gspec = lambda off: pl.BlockSpec((1, tn, D_MODEL), lambda j, i: (l, j + off, 0))
    return pl.pallas_call(
        _merge_kernel,
        out_shape=jax.ShapeDtypeStruct((m, D_MODEL), BF16),
        grid=(nn, m // tm),
        in_specs=[rows, rows, rows, gspec(0), gspec(nn), wspec(0), wspec(0)],
        out_specs=pl.BlockSpec((tm, tn), lambda j, i: (i, j)),
        compiler_params=_params("parallel", "parallel"),
        name="merge",
    )(h, out_a, out_b, w_gates, w_gates, w_pa, w_pb)


def _run_trunk(x, mods, wts, nb, t, tm, tr, chunk, caches):
    sample = caches is not None
    tm_wide = min(2 * tm, x.shape[0])
    tpg = max(t // tm, 1)
    h = _modulate(x, mods(0, 0, 0), mods(0, 0, 1), tm, tpg)
    new_k, new_v, new_ki, new_sv = [], [], [], []
    for l in range(DEPTH):
        ln = lambda s: (wts["ln_g"][l, s][None], wts["ln_b"][l, s][None])
        act = _ffn_up(h, wts["up"], l, 0, tm_wide)
        x, h = _proj_resid_ln(act, wts["down"], l, 0, x, mods(l, 0, 2), *ln(0),
                              (mods(l, 1, 0), mods(l, 1, 1)), 0.5, tm, t)
        p = _in_proj(h, wts["in"], l, tm_wide)
        outs = _gmlp(p, wts["sgu_g"][l][None], wts["sgu_b"][l][None], *wts["spatial"][l][chunk],
                     wts["ki_g"][l][None], wts["ki_b"][l][None], chunk, tr, sample)
        out_a, k4, v4, kbf, kiln, kibf, vx = outs[:7]
        if sample:
            out_b = _dsa_sample(p, kbf, vx, kibf, *caches, l, nb, t)
            new_sv.append(outs[7].reshape(nb, t, D_A))
        else:
            out_b = _dsa_prompt(p, kbf, vx, kibf, nb, t)
        z = _merge(h, out_a, out_b, wts["gates"], wts["pa"], wts["pb"], l, tm)
        x, h = _proj_resid_ln(z, wts["out"], l, 0, x, mods(l, 1, 2), *ln(1),
                              (mods(l, 2, 0), mods(l, 2, 1)), 1.0, tm, t)
        act = _ffn_up(h, wts["up"], l, 1, tm_wide)
        nxt = (mods(l + 1, 0, 0), mods(l + 1, 0, 1)) if l + 1 < DEPTH else None
        x, h = _proj_resid_ln(act, wts["down"], l, 1, x, mods(l, 2, 2), *ln(2), nxt, 0.5, tm, t)
        new_k.append(k4.reshape(nb, t, N_KV_HEADS, HEAD_DIM))
        new_v.append(v4.reshape(nb, t, N_KV_HEADS, HEAD_DIM))
        new_ki.append(kiln.reshape(nb, t, IDX_DIM))
    return x, jnp.stack(new_k), jnp.stack(new_v), jnp.stack(new_ki), (jnp.stack(new_sv) if sample else None)


def kernel(x_prompt, x_sample, c_prompt, c_sample, cache_k, cache_v, cache_kidx, w_ada, b_ada, w_ffn_up,
           w_ffn_down, w_in, sgu_ln_g, sgu_ln_b, w_spatial, b_spatial, kidx_ln_g, kidx_ln_b, w_branch_a,
           w_branch_b, w_out, ln_g, ln_b):
    tr = 256
    c_all = jnp.concatenate([c_prompt, c_sample, jnp.zeros((ADA_ROWS - BATCH - DEC_BATCH, D_MODEL), F32)], axis=0)
    ada = _ada(c_all, w_ada, b_ada)

    def mod_cols(l, sub, kind, rows):
        c0 = (sub * 3 + kind) * D_MODEL
        return ada[l, rows, c0:c0 + D_MODEL]

    def mods_prompt(l, sub, kind):
        return mod_cols(l, sub, kind, slice(0, BATCH))[:, None, :]

    def mods_sample(l, sub, kind):
        rows = mod_cols(l, sub, kind, slice(BATCH, BATCH + DEC_BATCH))
        return jnp.repeat(rows, DEC_SEQ, axis=0)[None]

    def spatial(l, chunk):
        reps = tr // chunk
        w = jnp.tile(w_spatial[l][:, :chunk, :chunk], (1, reps, reps))
        b = jnp.tile(jnp.repeat(jnp.transpose(b_spatial[l][:, :chunk]), GMLP_GDIM, axis=1), (reps, 1))
        return w, b

    w_in_t = jnp.swapaxes(w_in, 1, 2)
    wts = dict(
        up=w_ffn_up, **{"in": w_in_t},
        down=w_ffn_down.astype(BF16),
        out=w_out.astype(BF16)[:, None],
        pa=w_branch_a.astype(BF16), pb=w_branch_b.astype(BF16),
        gates=w_in_t[:, COL_GATES:COL_GATES + 2 * D_MODEL, :].astype(BF16),
        ln_g=ln_g, ln_b=ln_b, sgu_g=sgu_ln_g, sgu_b=sgu_ln_b, ki_g=kidx_ln_g, ki_b=kidx_ln_b,
        spatial=[{c: spatial(l, c) for c in (GMLP_CHUNK, DEC_SEQ)} for l in range(DEPTH)],
    )

    y_p, k_p, v_p, ki_p, _ = _run_trunk(x_prompt.reshape(BATCH * SEQ, D_MODEL), mods_prompt, wts,
                                        BATCH, SEQ, 512, tr, GMLP_CHUNK, None)
    m_s = DEC_BATCH * DEC_SEQ
    y_s, k_s, v_s, ki_s, sv_s = _run_trunk(x_sample.reshape(m_s, D_MODEL), mods_sample, wts,
                                           DEC_BATCH, DEC_SEQ, m_s, tr, DEC_SEQ, (cache_k, cache_v, cache_kidx))
    return (y_p.reshape(BATCH, SEQ, D_MODEL), y_s.reshape(DEC_BATCH, DEC_SEQ, D_MODEL),
            k_p, v_p, ki_p, k_s, v_s, ki_s, sv_s)
```

```python
import functools

import jax
import jax.numpy as jnp
from jax import lax
from jax.experimental import pallas as pl
from jax.experimental.pallas import tpu as pltpu

F32 = jnp.float32
BF16 = jnp.bfloat16

D_MODEL = 2048
BATCH = 4
SEQ = 2048
DEPTH = 4
DEC_BATCH = 8
DEC_SEQ = 32
PAST_LEN = 4096
CHUNK_SHIFT = 6
GMLP_CHUNK = 128
GMLP_GROUPS = 8
D_A = D_MODEL
GMLP_GDIM = D_A // GMLP_GROUPS
HEAD_DIM = 128
N_HEADS = D_MODEL // HEAD_DIM
N_KV_HEADS = 4
Q_PER_KV = N_HEADS // N_KV_HEADS
D_B = N_HEADS * HEAD_DIM
KV_WIDTH = N_KV_HEADS * HEAD_DIM
IDX_HEADS = 16
IDX_DIM = 128
IDX_WIDTH = IDX_HEADS * IDX_DIM
TOPK_MAX = 256
D_FF = 5632
ALPHA = (2 * DEPTH) ** 0.25
LN_EPS = 1e-5
IDX_SCALE = (IDX_HEADS * IDX_DIM) ** -0.5
ATTN_SCALE = HEAD_DIM ** -0.5
LOG2_E = 1.4426950408889634
IDX_KEY_CHUNK = 512
DSA_BUCKETS = 4
DOWN_ROWS = 256
ADA_WIDTH = 9 * D_MODEL
ADA_ROWS = 16

COL_U = 0
COL_SV = D_A
COL_Q = 2 * D_A
COL_K = COL_Q + D_B
COL_V = COL_K + KV_WIDTH
COL_QI = COL_V + KV_WIDTH
COL_KI = COL_QI + IDX_WIDTH
COL_WI = COL_KI + IDX_DIM
COL_GATES = COL_WI + IDX_HEADS
IN_TN = 1024
IN_MAIN_WIDTH = 10 * IN_TN

LANE = 128
VMEM_LIMIT = 56 * 1024 * 1024
NT_DIMS = (((1,), (1,)), ((), ()))


def _params(*sem):
    return pltpu.CompilerParams(dimension_semantics=sem, vmem_limit_bytes=VMEM_LIMIT)


def _layer_norm_rows(r, g, b):
    mu = jnp.mean(r, axis=-1, keepdims=True)
    rc = r - mu
    var = jnp.mean(rc * rc, axis=-1, keepdims=True)
    return rc * lax.rsqrt(var + LN_EPS) * g + b


def _key_to_float(key):
    return lax.bitcast_convert_type(key ^ ((key >> 31) & 0x7FFFFFFF), F32)


def _kth_largest_key(count_ge, shape, n_sel):
    zero = jnp.zeros(shape, jnp.int32)
    prefix = jnp.where(count_ge(zero) >= n_sel, zero, jnp.int32(-2 ** 31))

    def refine(step, prefix):
        cand = prefix | (jnp.int32(1 << 30) >> step)
        return jnp.where(count_ge(cand) >= n_sel, cand, prefix)

    return lax.fori_loop(0, 31, refine, prefix)


def _tie_break_block(score, tau, tied_f, running, need):
    w = score.shape[1]
    first = lax.broadcasted_iota(jnp.int32, (w, w), 0)
    second = lax.broadcasted_iota(jnp.int32, (w, w), 1)
    precedes = jnp.where(first < second, 1.0, 0.0).astype(BF16)
    rank = running + jnp.dot(tied_f.astype(BF16), precedes, preferred_element_type=F32)
    keep_tied = jnp.where(tied_f > 0.5, jnp.where(rank < need, 0.0, -jnp.inf), -jnp.inf)
    return jnp.where(score > tau, 0.0, keep_tied), running + jnp.sum(tied_f, axis=-1, keepdims=True)


def _ada_kernel(c_ref, w_ref, b_ref, o_ref):
    c = c_ref[...]
    s = (c * jax.nn.sigmoid(c)).astype(BF16)
    o_ref[0] = jnp.dot(s, w_ref[0].astype(BF16), preferred_element_type=F32) + b_ref[0]


def _ada(c_all, w_ada, b_ada):
    tn = 1024
    return pl.pallas_call(
        _ada_kernel,
        out_shape=jax.ShapeDtypeStruct((DEPTH, ADA_ROWS, ADA_WIDTH), F32),
        grid=(DEPTH, ADA_WIDTH // tn),
        in_specs=[
            pl.BlockSpec((ADA_ROWS, D_MODEL), lambda l, j: (0, 0)),
            pl.BlockSpec((1, D_MODEL, tn), lambda l, j: (l, 0, j)),
            pl.BlockSpec((1, 1, tn), lambda l, j: (l, 0, j)),
        ],
        out_specs=pl.BlockSpec((1, ADA_ROWS, tn), lambda l, j: (l, 0, j)),
        compiler_params=_params("parallel", "parallel"),
        name="ada",
    )(c_all, w_ada, b_ada.reshape(DEPTH, 1, ADA_WIDTH))


def _mod_kernel(x_ref, sh_ref, sc_ref, h_ref):
    h_ref[...] = (x_ref[...] * (1.0 + sc_ref[0]) + sh_ref[0]).astype(BF16)


def _mod_spec(mod, tiles_per_group):
    return pl.BlockSpec((1, mod.shape[1], D_MODEL), lambda i, *_: (i // tiles_per_group, 0, 0))


def _modulate(x, shift, scale, tm, tpg):
    m = x.shape[0]
    return pl.pallas_call(
        _mod_kernel,
        out_shape=jax.ShapeDtypeStruct((m, D_MODEL), BF16),
        grid=(m // tm,),
        in_specs=[pl.BlockSpec((tm, D_MODEL), lambda i: (i, 0)), _mod_spec(shift, tpg), _mod_spec(scale, tpg)],
        out_specs=pl.BlockSpec((tm, D_MODEL), lambda i: (i, 0)),
        compiler_params=_params("parallel"),
        name="modulate",
    )(x, shift, scale)


def _up_kernel(h_ref, wa_ref, wb_ref, o_ref, wa_s, wb_s):
    @pl.when(pl.program_id(1) == 0)
    def _():
        wa_s[...] = wa_ref[0, 0].astype(BF16)
        wb_s[...] = wb_ref[0, 0].astype(BF16)

    h = h_ref[...]
    a = jnp.dot(h, wa_s[...], preferred_element_type=F32)
    b = jnp.dot(h, wb_s[...], preferred_element_type=F32)
    o_ref[...] = (a * jax.nn.sigmoid(a) * b).astype(BF16)


def _ffn_up(h, w_up, l, s, tm):
    m = h.shape[0]
    tf = 512
    nf = D_FF // tf
    return pl.pallas_call(
        _up_kernel,
        out_shape=jax.ShapeDtypeStruct((m, D_FF), BF16),
        grid=(nf, m // tm),
        in_specs=[
            pl.BlockSpec((tm, D_MODEL), lambda j, i: (i, 0)),
            pl.BlockSpec((1, 1, D_MODEL, tf), lambda j, i: (l, s, 0, j)),
            pl.BlockSpec((1, 1, D_MODEL, tf), lambda j, i: (l, s, 0, j + nf)),
        ],
        out_specs=pl.BlockSpec((tm, tf), lambda j, i: (i, j)),
        scratch_shapes=[pltpu.VMEM((D_MODEL, tf), BF16), pltpu.VMEM((D_MODEL, tf), BF16)],
        compiler_params=_params("parallel", "arbitrary"),
        name="ffn_up",
    )(h, w_up, w_up)


def _down_kernel(*refs, gscale, emit_h):
    if emit_h:
        a_ref, w_ref, x_ref, g_ref, lg_ref, lb_ref, sh_ref, sc_ref, xo_ref, ho_ref = refs
    else:
        a_ref, w_ref, x_ref, g_ref, lg_ref, lb_ref, xo_ref = refs
    y = jnp.dot(a_ref[...], w_ref[0, 0], preferred_element_type=F32)
    r = ALPHA * x_ref[...] + (gscale * g_ref[0]) * y
    xn = _layer_norm_rows(r, lg_ref[...], lb_ref[...])
    xo_ref[...] = xn
    if emit_h:
        ho_ref[...] = (xn * (1.0 + sc_ref[0]) + sh_ref[0]).astype(BF16)


def _proj_resid_ln(a, w, l, s, x, gate, lg, lb, nxt, gscale, tm, rows_per_group):
    m, kdim = a.shape
    tm = min(tm, DOWN_ROWS) if kdim == D_FF else tm
    tpg = max(rows_per_group // tm, 1)
    emit_h = nxt is not None
    row = lambda i: (i, 0)
    const = lambda i: (0, 0)
    in_specs = [
        pl.BlockSpec((tm, kdim), row),
        pl.BlockSpec((1, 1, kdim, D_MODEL), lambda i: (l, s, 0, 0), pipeline_mode=pl.Buffered(1)),
        pl.BlockSpec((tm, D_MODEL), row),
        _mod_spec(gate, tpg),
        pl.BlockSpec((1, D_MODEL), const),
        pl.BlockSpec((1, D_MODEL), const),
    ]
    args = [a, w, x, gate, lg, lb]
    out_shape = [jax.ShapeDtypeStruct((m, D_MODEL), F32)]
    out_specs = [pl.BlockSpec((tm, D_MODEL), row)]
    if emit_h:
        in_specs += [_mod_spec(nxt[0], tpg), _mod_spec(nxt[1], tpg)]
        args += [nxt[0], nxt[1]]
        out_shape.append(jax.ShapeDtypeStruct((m, D_MODEL), BF16))
        out_specs.append(pl.BlockSpec((tm, D_MODEL), row))
    outs = pl.pallas_call(
        functools.partial(_down_kernel, gscale=gscale, emit_h=emit_h),
        out_shape=out_shape,
        grid=(m // tm,),
        in_specs=in_specs,
        out_specs=out_specs,
        compiler_params=_params("arbitrary"),
        name="proj_resid_ln",
    )(*args)
    return (outs[0], outs[1]) if emit_h else (outs[0], None)


def _in_proj_kernel(h_ref, w_ref, o_ref, w_s):
    @pl.when(pl.program_id(1) == 0)
    def _():
        w_s[...] = w_ref[0].T.astype(BF16)

    o_ref[...] = jnp.dot(h_ref[...], w_s[...], preferred_element_type=F32)


def _in_proj(h, w_in_t, l, tm):
    m = h.shape[0]
    return pl.pallas_call(
        _in_proj_kernel,
        out_shape=jax.ShapeDtypeStruct((m, IN_MAIN_WIDTH), F32),
        grid=(IN_MAIN_WIDTH // IN_TN, m // tm),
        in_specs=[pl.BlockSpec((tm, D_MODEL), lambda j, i: (i, 0)),
                  pl.BlockSpec((1, IN_TN, D_MODEL), lambda j, i: (l, j, 0))],
        out_specs=pl.BlockSpec((tm, IN_TN), lambda j, i: (i, j)),
        scratch_shapes=[pltpu.VMEM((D_MODEL, IN_TN), BF16)],
        compiler_params=_params("parallel", "arbitrary"),
        name="in_proj",
    )(h, w_in_t)


def _gmlp_kernel(*refs, tr, chunk, sample):
    (u_ref, sv_ref, k_ref, v_ref, ki_ref, sg_ref, sb_ref, ws_ref, bias_ref, kg_ref, kb_ref,
     oa_ref, k4_ref, v4_ref, kbf_ref, kiln_ref, kibf_ref, vx_ref) = refs[:18]
    u = jax.nn.gelu(u_ref[...])
    svn = _layer_norm_rows(jax.nn.gelu(sv_ref[...]), sg_ref[...], sb_ref[...])
    if sample:
        refs[18][...] = svn
    svb = svn.astype(BF16)
    row = lax.broadcasted_iota(jnp.int32, (tr, tr), 0)
    col = lax.broadcasted_iota(jnp.int32, (tr, tr), 1)
    shift = chunk.bit_length() - 1
    visible = ((row >> shift) == (col >> shift)) & (col <= row)
    for g in range(GMLP_GROUPS):
        cols = slice(g * GMLP_GDIM, (g + 1) * GMLP_GDIM)
        wm = jnp.where(visible, ws_ref[g], 0.0).astype(BF16)
        f = jnp.dot(wm, svb[:, cols], preferred_element_type=F32) + bias_ref[:, cols]
        oa_ref[:, cols] = (u[:, cols] * f).astype(BF16)
    k = k_ref[...]
    v = v_ref[...]
    for h in range(N_KV_HEADS):
        k4_ref[:, h, :] = k[:, h * HEAD_DIM:(h + 1) * HEAD_DIM]
        v4_ref[:, h, :] = v[:, h * HEAD_DIM:(h + 1) * HEAD_DIM]
    kbf_ref[...] = k.astype(BF16)
    vx_ref[...] = v.astype(BF16)
    kiln = _layer_norm_rows(ki_ref[...], kg_ref[...], kb_ref[...])
    kiln_ref[...] = kiln
    kibf_ref[...] = kiln.astype(BF16)


def _gmlp(p, sgu_g, sgu_b, w_tiled, bias_tiled, ki_g, ki_b, chunk, tr, sample):
    m = p.shape[0]
    row = lambda c: (lambda i: (i, c))
    const2 = lambda i: (0, 0)
    out_shape = [
        jax.ShapeDtypeStruct((m, D_A), BF16),
        jax.ShapeDtypeStruct((m, N_KV_HEADS, HEAD_DIM), F32),
        jax.ShapeDtypeStruct((m, N_KV_HEADS, HEAD_DIM), F32),
        jax.ShapeDtypeStruct((m, KV_WIDTH), BF16),
        jax.ShapeDtypeStruct((m, IDX_DIM), F32),
        jax.ShapeDtypeStruct((m, IDX_DIM), BF16),
        jax.ShapeDtypeStruct((m, KV_WIDTH), BF16),
    ]
    head4 = pl.BlockSpec((tr, N_KV_HEADS, HEAD_DIM), lambda i: (i, 0, 0))
    out_specs = [
        pl.BlockSpec((tr, D_A), row(0)), head4, head4,
        pl.BlockSpec((tr, KV_WIDTH), row(0)),
        pl.BlockSpec((tr, IDX_DIM), row(0)),
        pl.BlockSpec((tr, IDX_DIM), row(0)),
        pl.BlockSpec((tr, KV_WIDTH), row(0)),
    ]
    if sample:
        out_shape.append(jax.ShapeDtypeStruct((m, D_A), F32))
        out_specs.append(pl.BlockSpec((tr, D_A), row(0)))
    return pl.pallas_call(
        functools.partial(_gmlp_kernel, tr=tr, chunk=chunk, sample=sample),
        out_shape=out_shape,
        grid=(m // tr,),
        in_specs=[
            pl.BlockSpec((tr, D_A), row(COL_U // D_A)),
            pl.BlockSpec((tr, D_A), row(COL_SV // D_A)),
            pl.BlockSpec((tr, KV_WIDTH), row(COL_K // KV_WIDTH)),
            pl.BlockSpec((tr, KV_WIDTH), row(COL_V // KV_WIDTH)),
            pl.BlockSpec((tr, IDX_DIM), row(COL_KI // IDX_DIM)),
            pl.BlockSpec((1, D_A), const2),
            pl.BlockSpec((1, D_A), const2),
            pl.BlockSpec((GMLP_GROUPS, tr, tr), lambda i: (0, 0, 0)),
            pl.BlockSpec((tr, D_A), const2),
            pl.BlockSpec((1, IDX_DIM), const2),
            pl.BlockSpec((1, IDX_DIM), const2),
        ],
        out_specs=out_specs,
        compiler_params=_params("parallel"),
        name="gmlp",
    )(p, p, p, p, p, sgu_g, sgu_b, w_tiled, bias_tiled, ki_g, ki_b)


def _dsa_prompt_body(i, q_ref, qi0_ref, qi1_ref, wi_ref, k_ref, v_ref, ki_ref, o_ref, score_ref, score_t_ref,
                     bias_ref, *, tq, n_keys, n_sel):
    half = IDX_HEADS // 2
    wi = wi_ref[...]
    qi_rows = jnp.concatenate(
        [(qi0_ref if h < half else qi1_ref)[:, (h % half) * IDX_DIM:(h % half + 1) * IDX_DIM]
         for h in range(IDX_HEADS)], axis=0).astype(BF16)
    for c0 in range(0, n_keys, IDX_KEY_CHUNK):
        c1 = min(c0 + IDX_KEY_CHUNK, n_keys)
        logits = lax.dot_general(qi_rows, ki_ref[c0:c1, :], NT_DIMS, preferred_element_type=F32)
        total = jnp.maximum(logits[0:tq], 0.0) * wi[:, 0:1]
        for h in range(1, IDX_HEADS):
            total = total + jnp.maximum(logits[h * tq:(h + 1) * tq], 0.0) * wi[:, h:h + 1]
        score_ref[:, c0:c1] = total * IDX_SCALE
    qpos = i * tq + lax.broadcasted_iota(jnp.int32, (tq, n_keys), 0)
    kpos = lax.broadcasted_iota(jnp.int32, (tq, n_keys), 1)
    admissible = (kpos >> CHUNK_SHIFT) <= (qpos >> CHUNK_SHIFT)
    score_ref[:, 0:n_keys] = jnp.where(admissible, score_ref[:, 0:n_keys], -jnp.inf)
    n_adm = jnp.sum(jnp.where(admissible, 1.0, 0.0), axis=-1, keepdims=True)

    for c0 in range(0, n_keys, tq):
        score_t_ref[c0:c0 + tq, :] = score_ref[:, c0:c0 + tq].T

    def count_ge(key):
        cf = _key_to_float(key)
        part = jnp.where(score_t_ref[0:tq, :] >= cf, 1.0, 0.0)
        for r0 in range(tq, n_keys, tq):
            part = part + jnp.where(score_t_ref[r0:r0 + tq, :] >= cf, 1.0, 0.0)
        return jnp.sum(part, axis=0, keepdims=True)

    prefix = _kth_largest_key(count_ge, (1, tq), n_sel)
    tau_col = jnp.broadcast_to(_key_to_float(prefix), (tq, tq)).T[:, 0:1]
    tau = jnp.where(n_adm <= n_sel, -jnp.inf, tau_col)
    bias_ref[:, 0:n_keys] = jnp.where(admissible, jnp.where(score_ref[:, 0:n_keys] >= tau, 0.0, -jnp.inf), -jnp.inf)
    n_ge = jnp.sum(jnp.where(bias_ref[:, 0:n_keys] == 0.0, 1.0, 0.0), axis=-1, keepdims=True)

    @pl.when(jnp.max(n_ge) > n_sel)
    def _():
        need = n_sel - jnp.sum(jnp.where(score_ref[:, 0:n_keys] > tau, 1.0, 0.0), axis=-1, keepdims=True)
        running = jnp.zeros((tq, 1), F32)
        q_chunk = (i * tq + lax.broadcasted_iota(jnp.int32, (tq, tq), 0)) >> CHUNK_SHIFT
        for c0 in range(0, n_keys, tq):
            blk = score_ref[:, c0:c0 + tq]
            adm = ((c0 + lax.broadcasted_iota(jnp.int32, (tq, tq), 1)) >> CHUNK_SHIFT) <= q_chunk
            tied_f = jnp.where(adm, jnp.where(blk == tau, 1.0, 0.0), 0.0)
            bias_ref[:, c0:c0 + tq], running = _tie_break_block(blk, tau, tied_f, running, need)

    bias = bias_ref[:, 0:n_keys]

    for hk in range(N_KV_HEADS):
        heads = [hk * Q_PER_KV + g for g in range(Q_PER_KV)]
        cols = slice(hk * HEAD_DIM, (hk + 1) * HEAD_DIM)
        qs = jnp.concatenate([q_ref[:, h * HEAD_DIM:(h + 1) * HEAD_DIM] for h in heads], axis=0)
        qs = (qs * (ATTN_SCALE * LOG2_E)).astype(BF16)
        s = lax.dot_general(qs, k_ref[0:n_keys, cols], NT_DIMS, preferred_element_type=F32)
        ps, ls = [], []
        for g in range(Q_PER_KV):
            sg = s[g * tq:(g + 1) * tq] + bias
            pg = jnp.exp2(sg - jnp.max(sg, axis=-1, keepdims=True))
            ls.append(jnp.sum(pg, axis=-1, keepdims=True))
            ps.append(pg.astype(BF16))
        o = jnp.dot(jnp.concatenate(ps, axis=0), v_ref[0:n_keys, cols], preferred_element_type=F32)
        for g, h in enumerate(heads):
            o_ref[:, h * HEAD_DIM:(h + 1) * HEAD_DIM] = (o[g * tq:(g + 1) * tq] / ls[g]).astype(BF16)


def _dsa_prompt_kernel(*refs, tq, nq, n_sel, n_buckets):
    i = pl.program_id(1)
    per = nq // n_buckets
    for c in range(n_buckets):
        @pl.when((i >= c * per) & (i < (c + 1) * per))
        def _(c=c):
            _dsa_prompt_body(i, *refs, tq=tq, n_keys=(c + 1) * per * tq, n_sel=n_sel)


def _dsa_prompt(p, kbf, vbf, kibf, nb, t):
    tq = 128
    nq = t // tq
    n_sel = min(TOPK_MAX, t // 4)
    qrow = lambda c: (lambda b, i: (b * nq + i, c))
    return pl.pallas_call(
        functools.partial(_dsa_prompt_kernel, tq=tq, nq=nq, n_sel=n_sel, n_buckets=DSA_BUCKETS),
        out_shape=jax.ShapeDtypeStruct((nb * t, D_B), BF16),
        grid=(nb, nq),
        in_specs=[
            pl.BlockSpec((tq, D_B), qrow(COL_Q // D_B)),
            pl.BlockSpec((tq, IDX_WIDTH // 2), qrow(2 * COL_QI // IDX_WIDTH)),
            pl.BlockSpec((tq, IDX_WIDTH // 2), qrow(2 * COL_QI // IDX_WIDTH + 1)),
            pl.BlockSpec((tq, LANE), qrow(COL_WI // LANE)),
            pl.BlockSpec((t, KV_WIDTH), lambda b, i: (b, 0)),
            pl.BlockSpec((t, KV_WIDTH), lambda b, i: (b, 0)),
            pl.BlockSpec((t, IDX_DIM), lambda b, i: (b, 0)),
        ],
        out_specs=pl.BlockSpec((tq, D_B), lambda b, i: (b * nq + i, 0)),
        scratch_shapes=[pltpu.VMEM((tq, t), F32), pltpu.VMEM((t, tq), F32), pltpu.VMEM((tq, t), F32)],
        compiler_params=_params("parallel", "parallel"),
        name="dsa_prompt",
    )(p, p, p, p, kbf, vbf, kibf)


def _dsa_sample_kernel(q_ref, qi0_ref, qi1_ref, wi_ref, kn_ref, vn_ref, kin_ref, kp_ref, vp_ref, kip_ref,
                       o_ref, kp_s, vp_s, kip_s, sp_ref, bp_ref, bn_ref, *, tq, n_chunks, chunk_len, n_sel):
    c = pl.program_id(1)
    off = pl.multiple_of(c * chunk_len, chunk_len)
    for h in range(N_KV_HEADS):
        kp_s[h, pl.ds(off, chunk_len), :] = kp_ref[0, 0, pl.ds(h, chunk_len, stride=N_KV_HEADS), :].astype(BF16)
        vp_s[h, pl.ds(off, chunk_len), :] = vp_ref[0, 0, pl.ds(h, chunk_len, stride=N_KV_HEADS), :].astype(BF16)
    kip_s[pl.ds(off, chunk_len), :] = kip_ref[0, 0].astype(BF16)

    @pl.when(c == n_chunks - 1)
    def _():
        half = IDX_HEADS // 2
        wi = wi_ref[...]
        qi_rows = jnp.concatenate(
            [(qi0_ref if h < half else qi1_ref)[:, (h % half) * IDX_DIM:(h % half + 1) * IDX_DIM]
             for h in range(IDX_HEADS)], axis=0).astype(BF16)

        def index_scores(keys):
            logits = lax.dot_general(qi_rows, keys, NT_DIMS, preferred_element_type=F32)
            total = jnp.maximum(logits[0:tq], 0.0) * wi[:, 0:1]
            for h in range(1, IDX_HEADS):
                total = total + jnp.maximum(logits[h * tq:(h + 1) * tq], 0.0) * wi[:, h:h + 1]
            return total * IDX_SCALE

        for j in range(n_chunks):
            sp_ref[:, j * chunk_len:(j + 1) * chunk_len] = index_scores(kip_s[j * chunk_len:(j + 1) * chunk_len, :])
        s_new = index_scores(kin_ref[...])

        def count_ge_float(cf):
            return (jnp.sum(jnp.where(sp_ref[...] >= cf, 1.0, 0.0), axis=-1, keepdims=True)
                    + jnp.sum(jnp.where(s_new >= cf, 1.0, 0.0), axis=-1, keepdims=True))

        def count_ge(key):
            return count_ge_float(_key_to_float(key))

        tau = _key_to_float(_kth_largest_key(count_ge, (tq, 1), n_sel))
        bp_ref[...] = jnp.where(sp_ref[...] >= tau, 0.0, -jnp.inf)
        bn_ref[...] = jnp.where(s_new >= tau, 0.0, -jnp.inf)

        @pl.when(jnp.max(count_ge_float(tau)) > n_sel)
        def _():
            need = n_sel - (jnp.sum(jnp.where(sp_ref[...] > tau, 1.0, 0.0), axis=-1, keepdims=True)
                            + jnp.sum(jnp.where(s_new > tau, 1.0, 0.0), axis=-1, keepdims=True))
            running = jnp.zeros((tq, 1), F32)
            for c0 in range(0, PAST_LEN, LANE):
                blk = sp_ref[:, c0:c0 + LANE]
                bp_ref[:, c0:c0 + LANE], running = _tie_break_block(
                    blk, tau, jnp.where(blk == tau, 1.0, 0.0), running, need)
            bn_ref[...], _ = _tie_break_block(s_new, tau, jnp.where(s_new == tau, 1.0, 0.0), running, need)

        bias_p = bp_ref[...]
        bias_n = bn_ref[...]

        for hk in range(N_KV_HEADS):
            heads = [hk * Q_PER_KV + g for g in range(Q_PER_KV)]
            cols = slice(hk * HEAD_DIM, (hk + 1) * HEAD_DIM)
            qs = jnp.concatenate([q_ref[:, h * HEAD_DIM:(h + 1) * HEAD_DIM] for h in heads], axis=0).astype(BF16)
            s_p = lax.dot_general(qs, kp_s[hk], NT_DIMS, preferred_element_type=F32) * ATTN_SCALE
            s_n = lax.dot_general(qs, kn_ref[:, cols], NT_DIMS, preferred_element_type=F32) * ATTN_SCALE
            pps, pns, ls = [], [], []
            for g in range(Q_PER_KV):
                rows = slice(g * tq, (g + 1) * tq)
                sp_g = s_p[rows] + bias_p
                sn_g = s_n[rows] + bias_n
                mx = jnp.maximum(jnp.max(sp_g, axis=-1, keepdims=True), jnp.max(sn_g, axis=-1, keepdims=True))
                pp = jnp.exp(sp_g - mx)
                pn = jnp.exp(sn_g - mx)
                ls.append(jnp.sum(pp, axis=-1, keepdims=True) + jnp.sum(pn, axis=-1, keepdims=True))
                pps.append(pp.astype(BF16))
                pns.append(pn.astype(BF16))
            o = (jnp.dot(jnp.concatenate(pps, axis=0), vp_s[hk], preferred_element_type=F32)
                 + jnp.dot(jnp.concatenate(pns, axis=0), vn_ref[:, cols], preferred_element_type=F32))
            for g, h in enumerate(heads):
                o_ref[:, h * HEAD_DIM:(h + 1) * HEAD_DIM] = (o[g * tq:(g + 1) * tq] / ls[g]).astype(BF16)


def _dsa_sample(p, kbf, vbf, kibf, cache_k, cache_v, cache_kidx, l, nb, t):
    assert PAST_LEN % (1 << CHUNK_SHIFT) == 0 and t <= (1 << CHUNK_SHIFT)
    n_chunks = 4
    chunk_len = PAST_LEN // n_chunks
    n_sel = min(TOPK_MAX, (PAST_LEN + t) // 4)
    qrow = lambda c: (lambda b, j: (b, c))
    new = lambda b, j: (b, 0)
    past5 = pl.BlockSpec((1, 1, chunk_len * N_KV_HEADS, HEAD_DIM), lambda b, j: (l, b, j, 0))
    rows_per_batch = PAST_LEN * N_KV_HEADS
    cache_k = cache_k.reshape(DEPTH, nb, rows_per_batch, HEAD_DIM)
    cache_v = cache_v.reshape(DEPTH, nb, rows_per_batch, HEAD_DIM)
    return pl.pallas_call(
        functools.partial(_dsa_sample_kernel, tq=t, n_chunks=n_chunks, chunk_len=chunk_len, n_sel=n_sel),
        out_shape=jax.ShapeDtypeStruct((nb * t, D_B), BF16),
        grid=(nb, n_chunks),
        in_specs=[
            pl.BlockSpec((t, D_B), qrow(COL_Q // D_B)),
            pl.BlockSpec((t, IDX_WIDTH // 2), qrow(2 * COL_QI // IDX_WIDTH)),
            pl.BlockSpec((t, IDX_WIDTH // 2), qrow(2 * COL_QI // IDX_WIDTH + 1)),
            pl.BlockSpec((t, LANE), qrow(COL_WI // LANE)),
            pl.BlockSpec((t, KV_WIDTH), new),
            pl.BlockSpec((t, KV_WIDTH), new),
            pl.BlockSpec((t, IDX_DIM), new),
            past5, past5,
            pl.BlockSpec((1, 1, chunk_len, IDX_DIM), lambda b, j: (l, b, j, 0)),
        ],
        out_specs=pl.BlockSpec((t, D_B), new),
        scratch_shapes=[pltpu.VMEM((N_KV_HEADS, PAST_LEN, HEAD_DIM), BF16),
                        pltpu.VMEM((N_KV_HEADS, PAST_LEN, HEAD_DIM), BF16),
                        pltpu.VMEM((PAST_LEN, IDX_DIM), BF16),
                        pltpu.VMEM((t, PAST_LEN), F32),
                        pltpu.VMEM((t, PAST_LEN), F32),
                        pltpu.VMEM((t, t), F32)],
        compiler_params=_params("parallel", "arbitrary"),
        name="dsa_sample",
    )(p, p, p, p, kbf, vbf, kibf, cache_k, cache_v, cache_kidx)


def _merge_kernel(h_ref, a_ref, b_ref, wga_ref, wgb_ref, wa_ref, wb_ref, o_ref):
    h = h_ref[...]
    ga = jax.nn.sigmoid(lax.dot_general(h, wga_ref[0], NT_DIMS, preferred_element_type=F32))
    gb = jax.nn.sigmoid(lax.dot_general(h, wgb_ref[0], NT_DIMS, preferred_element_type=F32))
    ya = jnp.dot(a_ref[...], wa_ref[0], preferred_element_type=F32)
    yb = jnp.dot(b_ref[...], wb_ref[0], preferred_element_type=F32)
    o_ref[...] = (ga * ya + gb * yb).astype(BF16)


def _merge(h, out_a, out_b, w_gates, w_pa, w_pb, l, tm):
    m = h.shape[0]
    tn = 512
    nn = D_MODEL // tn
    rows = pl.BlockSpec((tm, D_MODEL), lambda j, i: (i, 0))
    wspec = lambda off: pl.BlockSpec((1, D_MODEL, tn), lambda j, i: (l, 0, j + off))
    gspec = lambda off: pl.BlockSpec((1, tn, D_MODEL), lambda j, i: (l, j + off, 0))
    return pl.pallas_call(
        _merge_kernel,
        out_shape=jax.ShapeDtypeStruct((m, D_MODEL), BF16),
        grid=(nn, m // tm),
        in_specs=[rows, rows, rows, gspec(0), gspec(nn), wspec(0), wspec(0)],
        out_specs=pl.BlockSpec((tm, tn), lambda j, i: (i, j)),
        compiler_params=_params("parallel", "parallel"),
        name="merge",
    )(h, out_a, out_b, w_gates, w_gates, w_pa, w_pb)


def _run_trunk(x, mods, wts, nb, t, tm, tr, chunk, caches):
    sample = caches is not None
    tm_wide = min(2 * tm, x.shape[0])
    tpg = max(t // tm, 1)
    h = _modulate(x, mods(0, 0, 0), mods(0, 0, 1), tm, tpg)
    new_k, new_v, new_ki, new_sv = [], [], [], []
    for l in range(DEPTH):
        ln = lambda s: (wts["ln_g"][l, s][None], wts["ln_b"][l, s][None])
        act = _ffn_up(h, wts["up"], l, 0, tm_wide)
        x, h = _proj_resid_ln(act, wts["down"], l, 0, x, mods(l, 0, 2), *ln(0),
                              (mods(l, 1, 0), mods(l, 1, 1)), 0.5, tm, t)
        p = _in_proj(h, wts["in"], l, tm_wide)
        outs = _gmlp(p, wts["sgu_g"][l][None], wts["sgu_b"][l][None], *wts["spatial"][l][chunk],
                     wts["ki_g"][l][None], wts["ki_b"][l][None], chunk, tr, sample)
        out_a, k4, v4, kbf, kiln, kibf, vx = outs[:7]
        if sample:
            out_b = _dsa_sample(p, kbf, vx, kibf, *caches, l, nb, t)
            new_sv.append(outs[7].reshape(nb, t, D_A))
        else:
            out_b = _dsa_prompt(p, kbf, vx, kibf, nb, t)
        z = _merge(h, out_a, out_b, wts["gates"], wts["pa"], wts["pb"], l, tm)
        x, h = _proj_resid_ln(z, wts["out"], l, 0, x, mods(l, 1, 2), *ln(1),
                              (mods(l, 2, 0), mods(l, 2, 1)), 1.0, tm, t)
        act = _ffn_up(h, wts["up"], l, 1, tm_wide)
        nxt = (mods(l + 1, 0, 0), mods(l + 1, 0, 1)) if l + 1 < DEPTH else None
        x, h = _proj_resid_ln(act, wts["down"], l, 1, x, mods(l, 2, 2), *ln(2), nxt, 0.5, tm, t)
        new_k.append(k4.reshape(nb, t, N_KV_HEADS, HEAD_DIM))
        new_v.append(v4.reshape(nb, t, N_KV_HEADS, HEAD_DIM))
        new_ki.append(kiln.reshape(nb, t, IDX_DIM))
    return x, jnp.stack(new_k), jnp.stack(new_v), jnp.stack(new_ki), (jnp.stack(new_sv) if sample else None)


def kernel(x_prompt, x_sample, c_prompt, c_sample, cache_k, cache_v, cache_kidx, w_ada, b_ada, w_ffn_up,
           w_ffn_down, w_in, sgu_ln_g, sgu_ln_b, w_spatial, b_spatial, kidx_ln_g, kidx_ln_b, w_branch_a,
           w_branch_b, w_out, ln_g, ln_b):
    tr = 256
    c_all = jnp.concatenate([c_prompt, c_sample, jnp.zeros((ADA_ROWS - BATCH - DEC_BATCH, D_MODEL), F32)], axis=0)
    ada = _ada(c_all, w_ada, b_ada)

    def mod_cols(l, sub, kind, rows):
        c0 = (sub * 3 + kind) * D_MODEL
        return ada[l, rows, c0:c0 + D_MODEL]

    def mods_prompt(l, sub, kind):
        return mod_cols(l, sub, kind, slice(0, BATCH))[:, None, :]

    def mods_sample(l, sub, kind):
        rows = mod_cols(l, sub, kind, slice(BATCH, BATCH + DEC_BATCH))
        return jnp.repeat(rows, DEC_SEQ, axis=0)[None]

    def spatial(l, chunk):
        reps = tr // chunk
        w = jnp.tile(w_spatial[l][:, :chunk, :chunk], (1, reps, reps))
        b = jnp.tile(jnp.repeat(jnp.transpose(b_spatial[l][:, :chunk]), GMLP_GDIM, axis=1), (reps, 1))
        return w, b

    w_in_t = jnp.swapaxes(w_in, 1, 2)
    wts = dict(
        up=w_ffn_up, **{"in": w_in_t},
        down=w_ffn_down.astype(BF16),
        out=w_out.astype(BF16)[:, None],
        pa=w_branch_a.astype(BF16), pb=w_branch_b.astype(BF16),
        gates=w_in_t[:, COL_GATES:COL_GATES + 2 * D_MODEL, :].astype(BF16),
        ln_g=ln_g, ln_b=ln_b, sgu_g=sgu_ln_g, sgu_b=sgu_ln_b, ki_g=kidx_ln_g, ki_b=kidx_ln_b,
        spatial=[{c: spatial(l, c) for c in (GMLP_CHUNK, DEC_SEQ)} for l in range(DEPTH)],
    )

    y_p, k_p, v_p, ki_p, _ = _run_trunk(x_prompt.reshape(BATCH * SEQ, D_MODEL), mods_prompt, wts,
                                        BATCH, SEQ, 512, tr, GMLP_CHUNK, None)
    m_s = DEC_BATCH * DEC_SEQ
    y_s, k_s, v_s, ki_s, sv_s = _run_trunk(x_sample.reshape(m_s, D_MODEL), mods_sample, wts,
                                           DEC_BATCH, DEC_SEQ, m_s, tr, DEC_SEQ, (cache_k, cache_v, cache_kidx))
    return (y_p.reshape(BATCH, SEQ, D_MODEL), y_s.reshape(DEC_BATCH, DEC_SEQ, D_MODEL),
            k_p, v_p, ki_p, k_s, v_s, ki_s, sv_s)
```

```python
import functools

import jax
import jax.numpy as jnp
from jax import lax
from jax.experimental import pallas as pl
from jax.experimental.pallas import tpu as pltpu

F32 = jnp.float32
BF16 = jnp.bfloat16

D_MODEL = 2048
BATCH = 4
SEQ = 2048
DEPTH = 4
DEC_BATCH = 8
DEC_SEQ = 32
PAST_LEN = 4096
CHUNK_SHIFT = 6
GMLP_CHUNK = 128
GMLP_GROUPS = 8
D_A = D_MODEL
GMLP_GDIM = D_A // GMLP_GROUPS
HEAD_DIM = 128
N_HEADS = D_MODEL // HEAD_DIM
N_KV_HEADS = 4
Q_PER_KV = N_HEADS // N_KV_HEADS
D_B = N_HEADS * HEAD_DIM
KV_WIDTH = N_KV_HEADS * HEAD_DIM
IDX_HEADS = 16
IDX_DIM = 128
IDX_WIDTH = IDX_HEADS * IDX_DIM
TOPK_MAX = 256
D_FF = 5632
ALPHA = (2 * DEPTH) ** 0.25
LN_EPS = 1e-5
IDX_SCALE = (IDX_HEADS * IDX_DIM) ** -0.5
ATTN_SCALE = HEAD_DIM ** -0.5
LOG2_E = 1.4426950408889634
IDX_KEY_CHUNK = 512
DSA_BUCKETS = 4
DOWN_ROWS = 256
ADA_WIDTH = 9 * D_MODEL
ADA_ROWS = 16

COL_U = 0
COL_SV = D_A
COL_Q = 2 * D_A
COL_K = COL_Q + D_B
COL_V = COL_K + KV_WIDTH
COL_QI = COL_V + KV_WIDTH
COL_KI = COL_QI + IDX_WIDTH
COL_WI = COL_KI + IDX_DIM
COL_GATES = COL_WI + IDX_HEADS
IN_TN = 1024
IN_MAIN_WIDTH = 10 * IN_TN

LANE = 128
VMEM_LIMIT = 56 * 1024 * 1024
NT_DIMS = (((1,), (1,)), ((), ()))


def _params(*sem):
    return pltpu.CompilerParams(dimension_semantics=sem, vmem_limit_bytes=VMEM_LIMIT)


def _layer_norm_rows(r, g, b):
    mu = jnp.mean(r, axis=-1, keepdims=True)
    rc = r - mu
    var = jnp.mean(rc * rc, axis=-1, keepdims=True)
    return rc * lax.rsqrt(var + LN_EPS) * g + b


def _key_to_float(key):
    return lax.bitcast_convert_type(key ^ ((key >> 31) & 0x7FFFFFFF), F32)


def _kth_largest_key(count_ge, shape, n_sel):
    zero = jnp.zeros(shape, jnp.int32)
    prefix = jnp.where(count_ge(zero) >= n_sel, zero, jnp.int32(-2 ** 31))

    def refine(step, prefix):
        cand = prefix | (jnp.int32(1 << 30) >> step)
        return jnp.where(count_ge(cand) >= n_sel, cand, prefix)

    return lax.fori_loop(0, 31, refine, prefix)


def _tie_break_block(score, tau, tied_f, running, need):
    w = score.shape[1]
    first = lax.broadcasted_iota(jnp.int32, (w, w), 0)
    second = lax.broadcasted_iota(jnp.int32, (w, w), 1)
    precedes = jnp.where(first < second, 1.0, 0.0).astype(BF16)
    rank = running + jnp.dot(tied_f.astype(BF16), precedes, preferred_element_type=F32)
    keep_tied = jnp.where(tied_f > 0.5, jnp.where(rank < need, 0.0, -jnp.inf), -jnp.inf)
    return jnp.where(score > tau, 0.0, keep_tied), running + jnp.sum(tied_f, axis=-1, keepdims=True)


def _ada_kernel(c_ref, w_ref, b_ref, o_ref):
    c = c_ref[...]
    s = (c * jax.nn.sigmoid(c)).astype(BF16)
    o_ref[0] = jnp.dot(s, w_ref[0].astype(BF16), preferred_element_type=F32) + b_ref[0]


def _ada(c_all, w_ada, b_ada):
    tn = 1024
    return pl.pallas_call(
        _ada_kernel,
        out_shape=jax.ShapeDtypeStruct((DEPTH, ADA_ROWS, ADA_WIDTH), F32),
        grid=(DEPTH, ADA_WIDTH // tn),
        in_specs=[
            pl.BlockSpec((ADA_ROWS, D_MODEL), lambda l, j: (0, 0)),
            pl.BlockSpec((1, D_MODEL, tn), lambda l, j: (l, 0, j)),
            pl.BlockSpec((1, 1, tn), lambda l, j: (l, 0, j)),
        ],
        out_specs=pl.BlockSpec((1, ADA_ROWS, tn), lambda l, j: (l, 0, j)),
        compiler_params=_params("parallel", "parallel"),
        name="ada",
    )(c_all, w_ada, b_ada.reshape(DEPTH, 1, ADA_WIDTH))


def _mod_kernel(x_ref, sh_ref, sc_ref, h_ref):
    h_ref[...] = (x_ref[...] * (1.0 + sc_ref[0]) + sh_ref[0]).astype(BF16)


def _mod_spec(mod, tiles_per_group):
    return pl.BlockSpec((1, mod.shape[1], D_MODEL), lambda i, *_: (i // tiles_per_group, 0, 0))


def _modulate(x, shift, scale, tm, tpg):
    m = x.shape[0]
    return pl.pallas_call(
        _mod_kernel,
        out_shape=jax.ShapeDtypeStruct((m, D_MODEL), BF16),
        grid=(m // tm,),
        in_specs=[pl.BlockSpec((tm, D_MODEL), lambda i: (i, 0)), _mod_spec(shift, tpg), _mod_spec(scale, tpg)],
        out_specs=pl.BlockSpec((tm, D_MODEL), lambda i: (i, 0)),
        compiler_params=_params("parallel"),
        name="modulate",
    )(x, shift, scale)


def _up_kernel(h_ref, wa_ref, wb_ref, o_ref, wa_s, wb_s):
    @pl.when(pl.program_id(1) == 0)
    def _():
        wa_s[...] = wa_ref[0, 0].astype(BF16)
        wb_s[...] = wb_ref[0, 0].astype(BF16)

    h = h_ref[...]
    a = jnp.dot(h, wa_s[...], preferred_element_type=F32)
    b = jnp.dot(h, wb_s[...], preferred_element_type=F32)
    o_ref[...] = (a * jax.nn.sigmoid(a) * b).astype(BF16)


def _ffn_up(h, w_up, l, s, tm):
    m = h.shape[0]
    tf = 512
    nf = D_FF // tf
    return pl.pallas_call(
        _up_kernel,
        out_shape=jax.ShapeDtypeStruct((m, D_FF), BF16),
        grid=(nf, m // tm),
        in_specs=[
            pl.BlockSpec((tm, D_MODEL), lambda j, i: (i, 0)),
            pl.BlockSpec((1, 1, D_MODEL, tf), lambda j, i: (l, s, 0, j)),
            pl.BlockSpec((1, 1, D_MODEL, tf), lambda j, i: (l, s, 0, j + nf)),
        ],
        out_specs=pl.BlockSpec((tm, tf), lambda j, i: (i, j)),
        scratch_shapes=[pltpu.VMEM((D_MODEL, tf), BF16), pltpu.VMEM((D_MODEL, tf), BF16)],
        compiler_params=_params("parallel", "arbitrary"),
        name="ffn_up",
    )(h, w_up, w_up)


def _down_kernel(*refs, gscale, emit_h):
    if emit_h:
        a_ref, w_ref, x_ref, g_ref, lg_ref, lb_ref, sh_ref, sc_ref, xo_ref, ho_ref = refs
    else:
        a_ref, w_ref, x_ref, g_ref, lg_ref, lb_ref, xo_ref = refs
    y = jnp.dot(a_ref[...], w_ref[0, 0], preferred_element_type=F32)
    r = ALPHA * x_ref[...] + (gscale * g_ref[0]) * y
    xn = _layer_norm_rows(r, lg_ref[...], lb_ref[...])
    xo_ref[...] = xn
    if emit_h:
        ho_ref[...] = (xn * (1.0 + sc_ref[0]) + sh_ref[0]).astype(BF16)


def _proj_resid_ln(a, w, l, s, x, gate, lg, lb, nxt, gscale, tm, rows_per_group):
    m, kdim = a.shape
    tm = min(tm, DOWN_ROWS) if kdim == D_FF else tm
    tpg = max(rows_per_group // tm, 1)
    emit_h = nxt is not None
    row = lambda i: (i, 0)
    const = lambda i: (0, 0)
    in_specs = [
        pl.BlockSpec((tm, kdim), row),
        pl.BlockSpec((1, 1, kdim, D_MODEL), lambda i: (l, s, 0, 0), pipeline_mode=pl.Buffered(1)),
        pl.BlockSpec((tm, D_MODEL), row),
        _mod_spec(gate, tpg),
        pl.BlockSpec((1, D_MODEL), const),
        pl.BlockSpec((1, D_MODEL), const),
    ]
    args = [a, w, x, gate, lg, lb]
    out_shape = [jax.ShapeDtypeStruct((m, D_MODEL), F32)]
    out_specs = [pl.BlockSpec((tm, D_MODEL), row)]
    if emit_h:
        in_specs += [_mod_spec(nxt[0], tpg), _mod_spec(nxt[1], tpg)]
        args += [nxt[0], nxt[1]]
        out_shape.append(jax.ShapeDtypeStruct((m, D_MODEL), BF16))
        out_specs.append(pl.BlockSpec((tm, D_MODEL), row))
    outs = pl.pallas_call(
        functools.partial(_down_kernel, gscale=gscale, emit_h=emit_h),
        out_shape=out_shape,
        grid=(m // tm,),
        in_specs=in_specs,
        out_specs=out_specs,
        compiler_params=_params("arbitrary"),
        name="proj_resid_ln",
    )(*args)
    return (outs[0], outs[1]) if emit_h else (outs[0], None)


def _in_proj_kernel(h_ref, w_ref, o_ref, w_s):
    @pl.when(pl.program_id(1) == 0)
    def _():
        w_s[...] = w_ref[0].T.astype(BF16)

    o_ref[...] = jnp.dot(h_ref[...], w_s[...], preferred_element_type=F32)


def _in_proj(h, w_in_t, l, tm):
    m = h.shape[0]
    return pl.pallas_call(
        _in_proj_kernel,
        out_shape=jax.ShapeDtypeStruct((m, IN_MAIN_WIDTH), F32),
        grid=(IN_MAIN_WIDTH // IN_TN, m // tm),
        in_specs=[pl.BlockSpec((tm, D_MODEL), lambda j, i: (i, 0)),
                  pl.BlockSpec((1, IN_TN, D_MODEL), lambda j, i: (l, j, 0))],
        out_specs=pl.BlockSpec((tm, IN_TN), lambda j, i: (i, j)),
        scratch_shapes=[pltpu.VMEM((D_MODEL, IN_TN), BF16)],
        compiler_params=_params("parallel", "arbitrary"),
        name="in_proj",
    )(h, w_in_t)


def _gmlp_kernel(*refs, tr, chunk, sample):
    (u_ref, sv_ref, k_ref, v_ref, ki_ref, sg_ref, sb_ref, ws_ref, bias_ref, kg_ref, kb_ref,
     oa_ref, k4_ref, v4_ref, kbf_ref, kiln_ref, kibf_ref, vx_ref) = refs[:18]
    u = jax.nn.gelu(u_ref[...])
    svn = _layer_norm_rows(jax.nn.gelu(sv_ref[...]), sg_ref[...], sb_ref[...])
    if sample:
        refs[18][...] = svn
    svb = svn.astype(BF16)
    row = lax.broadcasted_iota(jnp.int32, (tr, tr), 0)
    col = lax.broadcasted_iota(jnp.int32, (tr, tr), 1)
    shift = chunk.bit_length() - 1
    visible = ((row >> shift) == (col >> shift)) & (col <= row)
    for g in range(GMLP_GROUPS):
        cols = slice(g * GMLP_GDIM, (g + 1) * GMLP_GDIM)
        wm = jnp.where(visible, ws_ref[g], 0.0).astype(BF16)
        f = jnp.dot(wm, svb[:, cols], preferred_element_type=F32) + bias_ref[:, cols]
        oa_ref[:, cols] = (u[:, cols] * f).astype(BF16)
    k = k_ref[...]
    v = v_ref[...]
    for h in range(N_KV_HEADS):
        k4_ref[:, h, :] = k[:, h * HEAD_DIM:(h + 1) * HEAD_DIM]
        v4_ref[:, h, :] = v[:, h * HEAD_DIM:(h + 1) * HEAD_DIM]
    kbf_ref[...] = k.astype(BF16)
    vx_ref[...] = v.astype(BF16)
    kiln = _layer_norm_rows(ki_ref[...], kg_ref[...], kb_ref[...])
    kiln_ref[...] = kiln
    kibf_ref[...] = kiln.astype(BF16)


def _gmlp(p, sgu_g, sgu_b, w_tiled, bias_tiled, ki_g, ki_b, chunk, tr, sample):
    m = p.shape[0]
    row = lambda c: (lambda i: (i, c))
    const2 = lambda i: (0, 0)
    out_shape = [
        jax.ShapeDtypeStruct((m, D_A), BF16),
        jax.ShapeDtypeStruct((m, N_KV_HEADS, HEAD_DIM), F32),
        jax.ShapeDtypeStruct((m, N_KV_HEADS, HEAD_DIM), F32),
        jax.ShapeDtypeStruct((m, KV_WIDTH), BF16),
        jax.ShapeDtypeStruct((m, IDX_DIM), F32),
        jax.ShapeDtypeStruct((m, IDX_DIM), BF16),
        jax.ShapeDtypeStruct((m, KV_WIDTH), BF16),
    ]
    head4 = pl.BlockSpec((tr, N_KV_HEADS, HEAD_DIM), lambda i: (i, 0, 0))
    out_specs = [
        pl.BlockSpec((tr, D_A), row(0)), head4, head4,
        pl.BlockSpec((tr, KV_WIDTH), row(0)),
        pl.BlockSpec((tr, IDX_DIM), row(0)),
        pl.BlockSpec((tr, IDX_DIM), row(0)),
        pl.BlockSpec((tr, KV_WIDTH), row(0)),
    ]
    if sample:
        out_shape.append(jax.ShapeDtypeStruct((m, D_A), F32))
        out_specs.append(pl.BlockSpec((tr, D_A), row(0)))
    return pl.pallas_call(
        functools.partial(_gmlp_kernel, tr=tr, chunk=chunk, sample=sample),
        out_shape=out_shape,
        grid=(m // tr,),
        in_specs=[
            pl.BlockSpec((tr, D_A), row(COL_U // D_A)),
            pl.BlockSpec((tr, D_A), row(COL_SV // D_A)),
            pl.BlockSpec((tr, KV_WIDTH), row(COL_K // KV_WIDTH)),
            pl.BlockSpec((tr, KV_WIDTH), row(COL_V // KV_WIDTH)),
            pl.BlockSpec((tr, IDX_DIM), row(COL_KI // IDX_DIM)),
            pl.BlockSpec((1, D_A), const2),
            pl.BlockSpec((1, D_A), const2),
            pl.BlockSpec((GMLP_GROUPS, tr, tr), lambda i: (0, 0, 0)),
            pl.BlockSpec((tr, D_A), const2),
            pl.BlockSpec((1, IDX_DIM), const2),
            pl.BlockSpec((1, IDX_DIM), const2),
        ],
        out_specs=out_specs,
        compiler_params=_params("parallel"),
        name="gmlp",
    )(p, p, p, p, p, sgu_g, sgu_b, w_tiled, bias_tiled, ki_g, ki_b)


def _dsa_prompt_body(i, q_ref, qi0_ref, qi1_ref, wi_ref, k_ref, v_ref, ki_ref, o_ref, score_ref, score_t_ref,
                     bias_ref, *, tq, n_keys, n_sel):
    half = IDX_HEADS // 2
    wi = wi_ref[...]
    qi_rows = jnp.concatenate(
        [(qi0_ref if h < half else qi1_ref)[:, (h % half) * IDX_DIM:(h % half + 1) * IDX_DIM]
         for h in range(IDX_HEADS)], axis=0).astype(BF16)
    for c0 in range(0, n_keys, IDX_KEY_CHUNK):
        c1 = min(c0 + IDX_KEY_CHUNK, n_keys)
        logits = lax.dot_general(qi_rows, ki_ref[c0:c1, :], NT_DIMS, preferred_element_type=F32)
        total = jnp.maximum(logits[0:tq], 0.0) * wi[:, 0:1]
        for h in range(1, IDX_HEADS):
            total = total + jnp.maximum(logits[h * tq:(h + 1) * tq], 0.0) * wi[:, h:h + 1]
        score_ref[:, c0:c1] = total * IDX_SCALE
    qpos = i * tq + lax.broadcasted_iota(jnp.int32, (tq, n_keys), 0)
    kpos = lax.broadcasted_iota(jnp.int32, (tq, n_keys), 1)
    admissible = (kpos >> CHUNK_SHIFT) <= (qpos >> CHUNK_SHIFT)
    score_ref[:, 0:n_keys] = jnp.where(admissible, score_ref[:, 0:n_keys], -jnp.inf)
    n_adm = jnp.sum(jnp.where(admissible, 1.0, 0.0), axis=-1, keepdims=True)

    for c0 in range(0, n_keys, tq):
        score_t_ref[c0:c0 + tq, :] = score_ref[:, c0:c0 + tq].T

    def count_ge(key):
        cf = _key_to_float(key)
        part = jnp.where(score_t_ref[0:tq, :] >= cf, 1.0, 0.0)
        for r0 in range(tq, n_keys, tq):
            part = part + jnp.where(score_t_ref[r0:r0 + tq, :] >= cf, 1.0, 0.0)
        return jnp.sum(part, axis=0, keepdims=True)

    prefix = _kth_largest_key(count_ge, (1, tq), n_sel)
    tau_col = jnp.broadcast_to(_key_to_float(prefix), (tq, tq)).T[:, 0:1]
    tau = jnp.where(n_adm <= n_sel, -jnp.inf, tau_col)
    bias_ref[:, 0:n_keys] = jnp.where(admissible, jnp.where(score_ref[:, 0:n_keys] >= tau, 0.0, -jnp.inf), -jnp.inf)
    n_ge = jnp.sum(jnp.where(bias_ref[:, 0:n_keys] == 0.0, 1.0, 0.0), axis=-1, keepdims=True)

    @pl.when(jnp.max(n_ge) > n_sel)
    def _():
        need = n_sel - jnp.sum(jnp.where(score_ref[:, 0:n_keys] > tau, 1.0, 0.0), axis=-1, keepdims=True)
        running = jnp.zeros((tq, 1), F32)
        q_chunk = (i * tq + lax.broadcasted_iota(jnp.int32, (tq, tq), 0)) >> CHUNK_SHIFT
        for c0 in range(0, n_keys, tq):
            blk = score_ref[:, c0:c0 + tq]
            adm = ((c0 + lax.broadcasted_iota(jnp.int32, (tq, tq), 1)) >> CHUNK_SHIFT) <= q_chunk
            tied_f = jnp.where(adm, jnp.where(blk == tau, 1.0, 0.0), 0.0)
            bias_ref[:, c0:c0 + tq], running = _tie_break_block(blk, tau, tied_f, running, need)

    bias = bias_ref[:, 0:n_keys]

    for hk in range(N_KV_HEADS):
        heads = [hk * Q_PER_KV + g for g in range(Q_PER_KV)]
        cols = slice(hk * HEAD_DIM, (hk + 1) * HEAD_DIM)
        qs = jnp.concatenate([q_ref[:, h * HEAD_DIM:(h + 1) * HEAD_DIM] for h in heads], axis=0)
        qs = (qs * (ATTN_SCALE * LOG2_E)).astype(BF16)
        s = lax.dot_general(qs, k_ref[0:n_keys, cols], NT_DIMS, preferred_element_type=F32)
        ps, ls = [], []
        for g in range(Q_PER_KV):
            sg = s[g * tq:(g + 1) * tq] + bias
            pg = jnp.exp2(sg - jnp.max(sg, axis=-1, keepdims=True))
            ls.append(jnp.sum(pg, axis=-1, keepdims=True))
            ps.append(pg.astype(BF16))
        o = jnp.dot(jnp.concatenate(ps, axis=0), v_ref[0:n_keys, cols], preferred_element_type=F32)
        for g, h in enumerate(heads):
            o_ref[:, h * HEAD_DIM:(h + 1) * HEAD_DIM] = (o[g * tq:(g + 1) * tq] / ls[g]).astype(BF16)


def _dsa_prompt_kernel(*refs, tq, nq, n_sel, n_buckets):
    i = pl.program_id(1)
    per = nq // n_buckets
    for c in range(n_buckets):
        @pl.when((i >= c * per) & (i < (c + 1) * per))
        def _(c=c):
            _dsa_prompt_body(i, *refs, tq=tq, n_keys=(c + 1) * per * tq, n_sel=n_sel)


def _dsa_prompt(p, kbf, vbf, kibf, nb, t):
    tq = 128
    nq = t // tq
    n_sel = min(TOPK_MAX, t // 4)
    qrow = lambda c: (lambda b, i: (b * nq + i, c))
    return pl.pallas_call(
        functools.partial(_dsa_prompt_kernel, tq=tq, nq=nq, n_sel=n_sel, n_buckets=DSA_BUCKETS),
        out_shape=jax.ShapeDtypeStruct((nb * t, D_B), BF16),
        grid=(nb, nq),
        in_specs=[
            pl.BlockSpec((tq, D_B), qrow(COL_Q // D_B)),
            pl.BlockSpec((tq, IDX_WIDTH // 2), qrow(2 * COL_QI // IDX_WIDTH)),
            pl.BlockSpec((tq, IDX_WIDTH // 2), qrow(2 * COL_QI // IDX_WIDTH + 1)),
            pl.BlockSpec((tq, LANE), qrow(COL_WI // LANE)),
            pl.BlockSpec((t, KV_WIDTH), lambda b, i: (b, 0)),
            pl.BlockSpec((t, KV_WIDTH), lambda b, i: (b, 0)),
            pl.BlockSpec((t, IDX_DIM), lambda b, i: (b, 0)),
        ],
        out_specs=pl.BlockSpec((tq, D_B), lambda b, i: (b * nq + i, 0)),
        scratch_shapes=[pltpu.VMEM((tq, t), F32), pltpu.VMEM((t, tq), F32), pltpu.VMEM((tq, t), F32)],
        compiler_params=_params("parallel", "parallel"),
        name="dsa_prompt",
    )(p, p, p, p, kbf, vbf, kibf)


def _dsa_sample_kernel(q_ref, qi0_ref, qi1_ref, wi_ref, kn_ref, vn_ref, kin_ref, kp_ref, vp_ref, kip_ref,
                       o_ref, kp_s, vp_s, spc_ref, sp_ref, bp_ref, bn_ref, *, tq, n_chunks, chunk_len, n_sel):
    c = pl.program_id(1)
    off = pl.multiple_of(c * chunk_len, chunk_len)
    for h in range(N_KV_HEADS):
        kp_s[h, pl.ds(off, chunk_len), :] = kp_ref[0, 0, pl.ds(h, chunk_len, stride=N_KV_HEADS), :].astype(BF16)
        vp_s[h, pl.ds(off, chunk_len), :] = vp_ref[0, 0, pl.ds(h, chunk_len, stride=N_KV_HEADS), :].astype(BF16)

    half = IDX_HEADS // 2
    wi = wi_ref[...]
    qi_rows = jnp.concatenate(
        [(qi0_ref if h < half else qi1_ref)[:, (h % half) * IDX_DIM:(h % half + 1) * IDX_DIM]
         for h in range(IDX_HEADS)], axis=0).astype(BF16)

    def index_scores(keys):
        logits = lax.dot_general(qi_rows, keys, NT_DIMS, preferred_element_type=F32)
        total = jnp.maximum(logits[0:tq], 0.0) * wi[:, 0:1]
        for h in range(1, IDX_HEADS):
            total = total + jnp.maximum(logits[h * tq:(h + 1) * tq], 0.0) * wi[:, h:h + 1]
        return total * IDX_SCALE

    spc_ref[c] = index_scores(kip_ref[0, 0].astype(BF16))

    @pl.when(c == n_chunks - 1)
    def _():
        for j in range(n_chunks):
            sp_ref[:, j * chunk_len:(j + 1) * chunk_len] = spc_ref[j]
        s_new = index_scores(kin_ref[...])

        def count_ge_float(cf):
            return (jnp.sum(jnp.where(sp_ref[...] >= cf, 1.0, 0.0), axis=-1, keepdims=True)
                    + jnp.sum(jnp.where(s_new >= cf, 1.0, 0.0), axis=-1, keepdims=True))

        def count_ge(key):
            return count_ge_float(_key_to_float(key))

        tau = _key_to_float(_kth_largest_key(count_ge, (tq, 1), n_sel))
        bp_ref[...] = jnp.where(sp_ref[...] >= tau, 0.0, -jnp.inf)
        bn_ref[...] = jnp.where(s_new >= tau, 0.0, -jnp.inf)

        @pl.when(jnp.max(count_ge_float(tau)) > n_sel)
        def _():
            need = n_sel - (jnp.sum(jnp.where(sp_ref[...] > tau, 1.0, 0.0), axis=-1, keepdims=True)
                            + jnp.sum(jnp.where(s_new > tau, 1.0, 0.0), axis=-1, keepdims=True))
            running = jnp.zeros((tq, 1), F32)
            for c0 in range(0, PAST_LEN, LANE):
                blk = sp_ref[:, c0:c0 + LANE]
                bp_ref[:, c0:c0 + LANE], running = _tie_break_block(
                    blk, tau, jnp.where(blk == tau, 1.0, 0.0), running, need)
            bn_ref[...], _ = _tie_break_block(s_new, tau, jnp.where(s_new == tau, 1.0, 0.0), running, need)

        bias_p = bp_ref[...]
        bias_n = bn_ref[...]

        for hk in range(N_KV_HEADS):
            heads = [hk * Q_PER_KV + g for g in range(Q_PER_KV)]
            cols = slice(hk * HEAD_DIM, (hk + 1) * HEAD_DIM)
            qs = jnp.concatenate([q_ref[:, h * HEAD_DIM:(h + 1) * HEAD_DIM] for h in heads], axis=0).astype(BF16)
            s_p = lax.dot_general(qs, kp_s[hk], NT_DIMS, preferred_element_type=F32) * ATTN_SCALE
            s_n = lax.dot_general(qs, kn_ref[:, cols], NT_DIMS, preferred_element_type=F32) * ATTN_SCALE
            pps, pns, ls = [], [], []
            for g in range(Q_PER_KV):
                rows = slice(g * tq, (g + 1) * tq)
                sp_g = s_p[rows] + bias_p
                sn_g = s_n[rows] + bias_n
                mx = jnp.maximum(jnp.max(sp_g, axis=-1, keepdims=True), jnp.max(sn_g, axis=-1, keepdims=True))
                pp = jnp.exp(sp_g - mx)
                pn = jnp.exp(sn_g - mx)
                ls.append(jnp.sum(pp, axis=-1, keepdims=True) + jnp.sum(pn, axis=-1, keepdims=True))
                pps.append(pp.astype(BF16))
                pns.append(pn.astype(BF16))
            o = (jnp.dot(jnp.concatenate(pps, axis=0), vp_s[hk], preferred_element_type=F32)
                 + jnp.dot(jnp.concatenate(pns, axis=0), vn_ref[:, cols], preferred_element_type=F32))
            for g, h in enumerate(heads):
                o_ref[:, h * HEAD_DIM:(h + 1) * HEAD_DIM] = (o[g * tq:(g + 1) * tq] / ls[g]).astype(BF16)


def _dsa_sample(p, kbf, vbf, kibf, cache_k, cache_v, cache_kidx, l, nb, t):
    assert PAST_LEN % (1 << CHUNK_SHIFT) == 0 and t <= (1 << CHUNK_SHIFT)
    n_chunks = 4
    chunk_len = PAST_LEN // n_chunks
    n_sel = min(TOPK_MAX, (PAST_LEN + t) // 4)
    qrow = lambda c: (lambda b, j: (b, c))
    new = lambda b, j: (b, 0)
    past5 = pl.BlockSpec((1, 1, chunk_len * N_KV_HEADS, HEAD_DIM), lambda b, j: (l, b, j, 0))
    rows_per_batch = PAST_LEN * N_KV_HEADS
    cache_k = cache_k.reshape(DEPTH, nb, rows_per_batch, HEAD_DIM)
    cache_v = cache_v.reshape(DEPTH, nb, rows_per_batch, HEAD_DIM)
    return pl.pallas_call(
        functools.partial(_dsa_sample_kernel, tq=t, n_chunks=n_chunks, chunk_len=chunk_len, n_sel=n_sel),
        out_shape=jax.ShapeDtypeStruct((nb * t, D_B), BF16),
        grid=(nb, n_chunks),
        in_specs=[
            pl.BlockSpec((t, D_B), qrow(COL_Q // D_B)),
            pl.BlockSpec((t, IDX_WIDTH // 2), qrow(2 * COL_QI // IDX_WIDTH)),
            pl.BlockSpec((t, IDX_WIDTH // 2), qrow(2 * COL_QI // IDX_WIDTH + 1)),
            pl.BlockSpec((t, LANE), qrow(COL_WI // LANE)),
            pl.BlockSpec((t, KV_WIDTH), new),
            pl.BlockSpec((t, KV_WIDTH), new),
            pl.BlockSpec((t, IDX_DIM), new),
            past5, past5,
            pl.BlockSpec((1, 1, chunk_len, IDX_DIM), lambda b, j: (l, b, j, 0)),
        ],
        out_specs=pl.BlockSpec((t, D_B), new),
        scratch_shapes=[pltpu.VMEM((N_KV_HEADS, PAST_LEN, HEAD_DIM), BF16),
                        pltpu.VMEM((N_KV_HEADS, PAST_LEN, HEAD_DIM), BF16),
                        pltpu.VMEM((n_chunks, t, chunk_len), F32),
                        pltpu.VMEM((t, PAST_LEN), F32),
                        pltpu.VMEM((t, PAST_LEN), F32),
                        pltpu.VMEM((t, t), F32)],
        compiler_params=_params("parallel", "arbitrary"),
        name="dsa_sample",
    )(p, p, p, p, kbf, vbf, kibf, cache_k, cache_v, cache_kidx)


def _merge_kernel(h_ref, a_ref, b_ref, wga_ref, wgb_ref, wa_ref, wb_ref, o_ref):
    h = h_ref[...]
    ga = jax.nn.sigmoid(lax.dot_general(h, wga_ref[0], NT_DIMS, preferred_element_type=F32))
    gb = jax.nn.sigmoid(lax.dot_general(h, wgb_ref[0], NT_DIMS, preferred_element_type=F32))
    ya = jnp.dot(a_ref[...], wa_ref[0], preferred_element_type=F32)
    yb = jnp.dot(b_ref[...], wb_ref[0], preferred_element_type=F32)
    o_ref[...] = (ga * ya + gb * yb).astype(BF16)


def _merge(h, out_a, out_b, w_gates, w_pa, w_pb, l, tm):
    m = h.shape[0]
    tn = 512
    nn = D_MODEL // tn
    rows = pl.BlockSpec((tm, D_MODEL), lambda j, i: (i, 0))
    wspec = lambda off: pl.BlockSpec((1, D_MODEL, tn), lambda j, i: (l, 0, j + off))
    gspec = lambda off: pl.BlockSpec((1, tn, D_MODEL), lambda j, i: (l, j + off, 0))
    return pl.pallas_call(
        _merge_kernel,
        out_shape=jax.ShapeDtypeStruct((m, D_MODEL), BF16),
        grid=(nn, m // tm),
        in_specs=[rows, rows, rows, gspec(0), gspec(nn), wspec(0), wspec(0)],
        out_specs=pl.BlockSpec((tm, tn), lambda j, i: (i, j)),
        compiler_params=_params("parallel", "parallel"),
        name="merge",
    )(h, out_a, out_b, w_gates, w_gates, w_pa, w_pb)


def _run_trunk(x, mods, wts, nb, t, tm, tr, chunk, caches):
    sample = caches is not None
    tm_wide = min(2 * tm, x.shape[0])
    tpg = max(t // tm, 1)
    h = _modulate(x, mods(0, 0, 0), mods(0, 0, 1), tm, tpg)
    new_k, new_v, new_ki, new_sv = [], [], [], []
    for l in range(DEPTH):
        ln = lambda s: (wts["ln_g"][l, s][None], wts["ln_b"][l, s][None])
        act = _ffn_up(h, wts["up"], l, 0, tm_wide)
        x, h = _proj_resid_ln(act, wts["down"], l, 0, x, mods(l, 0, 2), *ln(0),
                              (mods(l, 1, 0), mods(l, 1, 1)), 0.5, tm, t)
        p = _in_proj(h, wts["in"], l, tm_wide)
        outs = _gmlp(p, wts["sgu_g"][l][None], wts["sgu_b"][l][None], *wts["spatial"][l][chunk],
                     wts["ki_g"][l][None], wts["ki_b"][l][None], chunk, tr, sample)
        out_a, k4, v4, kbf, kiln, kibf, vx = outs[:7]
        if sample:
            out_b = _dsa_sample(p, kbf, vx, kibf, *caches, l, nb, t)
            new_sv.append(outs[7].reshape(nb, t, D_A))
        else:
            out_b = _dsa_prompt(p, kbf, vx, kibf, nb, t)
        z = _merge(h, out_a, out_b, wts["gates"], wts["pa"], wts["pb"], l, tm)
        x, h = _proj_resid_ln(z, wts["out"], l, 0, x, mods(l, 1, 2), *ln(1),
                              (mods(l, 2, 0), mods(l, 2, 1)), 1.0, tm, t)
        act = _ffn_up(h, wts["up"], l, 1, tm_wide)
        nxt = (mods(l + 1, 0, 0), mods(l + 1, 0, 1)) if l + 1 < DEPTH else None
        x, h = _proj_resid_ln(act, wts["down"], l, 1, x, mods(l, 2, 2), *ln(2), nxt, 0.5, tm, t)
        new_k.append(k4.reshape(nb, t, N_KV_HEADS, HEAD_DIM))
        new_v.append(v4.reshape(nb, t, N_KV_HEADS, HEAD_DIM))
        new_ki.append(kiln.reshape(nb, t, IDX_DIM))
    return x, jnp.stack(new_k), jnp.stack(new_v), jnp.stack(new_ki), (jnp.stack(new_sv) if sample else None)


def kernel(x_prompt, x_sample, c_prompt, c_sample, cache_k, cache_v, cache_kidx, w_ada, b_ada, w_ffn_up,
           w_ffn_down, w_in, sgu_ln_g, sgu_ln_b, w_spatial, b_spatial, kidx_ln_g, kidx_ln_b, w_branch_a,
           w_branch_b, w_out, ln_g, ln_b):
    tr = 256
    c_all = jnp.concatenate([c_prompt, c_sample, jnp.zeros((ADA_ROWS - BATCH - DEC_BATCH, D_MODEL), F32)], axis=0)
    ada = _ada(c_all, w_ada, b_ada)

    def mod_cols(l, sub, kind, rows):
        c0 = (sub * 3 + kind) * D_MODEL
        return ada[l, rows, c0:c0 + D_MODEL]

    def mods_prompt(l, sub, kind):
        return mod_cols(l, sub, kind, slice(0, BATCH))[:, None, :]

    def mods_sample(l, sub, kind):
        rows = mod_cols(l, sub, kind, slice(BATCH, BATCH + DEC_BATCH))
        return jnp.repeat(rows, DEC_SEQ, axis=0)[None]

    def spatial(l, chunk):
        reps = tr // chunk
        w = jnp.tile(w_spatial[l][:, :chunk, :chunk], (1, reps, reps))
        b = jnp.tile(jnp.repeat(jnp.transpose(b_spatial[l][:, :chunk]), GMLP_GDIM, axis=1), (reps, 1))
        return w, b

    w_in_t = jnp.swapaxes(w_in, 1, 2)
    wts = dict(
        up=w_ffn_up, **{"in": w_in_t},
        down=w_ffn_down.astype(BF16),
        out=w_out.astype(BF16)[:, None],
        pa=w_branch_a.astype(BF16), pb=w_branch_b.astype(BF16),
        gates=w_in_t[:, COL_GATES:COL_GATES + 2 * D_MODEL, :].astype(BF16),
        ln_g=ln_g, ln_b=ln_b, sgu_g=sgu_ln_g, sgu_b=sgu_ln_b, ki_g=kidx_ln_g, ki_b=kidx_ln_b,
        spatial=[{c: spatial(l, c) for c in (GMLP_CHUNK, DEC_SEQ)} for l in range(DEPTH)],
    )

    y_p, k_p, v_p, ki_p, _ = _run_trunk(x_prompt.reshape(BATCH * SEQ, D_MODEL), mods_prompt, wts,
                                        BATCH, SEQ, 512, tr, GMLP_CHUNK, None)
    m_s = DEC_BATCH * DEC_SEQ
    y_s, k_s, v_s, ki_s, sv_s = _run_trunk(x_sample.reshape(m_s, D_MODEL), mods_sample, wts,
                                           DEC_BATCH, DEC_SEQ, m_s, tr, DEC_SEQ, (cache_k, cache_v, cache_kidx))
    return (y_p.reshape(BATCH, SEQ, D_MODEL), y_s.reshape(DEC_BATCH, DEC_SEQ, D_MODEL),
            k_p, v_p, ki_p, k_s, v_s, ki_s, sv_s)
```

```python
import functools

import jax
import jax.numpy as jnp
from jax import lax
from jax.experimental import pallas as pl
from jax.experimental.pallas import tpu as pltpu

F32 = jnp.float32
BF16 = jnp.bfloat16

D_MODEL = 2048
BATCH = 4
SEQ = 2048
DEPTH = 4
DEC_BATCH = 8
DEC_SEQ = 32
PAST_LEN = 4096
CHUNK_SHIFT = 6
GMLP_CHUNK = 128
GMLP_GROUPS = 8
D_A = D_MODEL
GMLP_GDIM = D_A // GMLP_GROUPS
HEAD_DIM = 128
N_HEADS = D_MODEL // HEAD_DIM
N_KV_HEADS = 4
Q_PER_KV = N_HEADS // N_KV_HEADS
D_B = N_HEADS * HEAD_DIM
KV_WIDTH = N_KV_HEADS * HEAD_DIM
IDX_HEADS = 16
IDX_DIM = 128
IDX_WIDTH = IDX_HEADS * IDX_DIM
TOPK_MAX = 256
D_FF = 5632
ALPHA = (2 * DEPTH) ** 0.25
LN_EPS = 1e-5
IDX_SCALE = (IDX_HEADS * IDX_DIM) ** -0.5
ATTN_SCALE = HEAD_DIM ** -0.5
LOG2_E = 1.4426950408889634
IDX_KEY_CHUNK = 512
DSA_BUCKETS = 4
DOWN_ROWS = 256
ADA_WIDTH = 9 * D_MODEL
ADA_ROWS = 16

COL_U = 0
COL_SV = D_A
COL_Q = 2 * D_A
COL_K = COL_Q + D_B
COL_V = COL_K + KV_WIDTH
COL_QI = COL_V + KV_WIDTH
COL_KI = COL_QI + IDX_WIDTH
COL_WI = COL_KI + IDX_DIM
COL_GATES = COL_WI + IDX_HEADS
IN_TN = 1024
IN_MAIN_WIDTH = 10 * IN_TN

LANE = 128
VMEM_LIMIT = 56 * 1024 * 1024
NT_DIMS = (((1,), (1,)), ((), ()))


def _params(*sem):
    return pltpu.CompilerParams(dimension_semantics=sem, vmem_limit_bytes=VMEM_LIMIT)


def _layer_norm_rows(r, g, b):
    mu = jnp.mean(r, axis=-1, keepdims=True)
    rc = r - mu
    var = jnp.mean(rc * rc, axis=-1, keepdims=True)
    return rc * lax.rsqrt(var + LN_EPS) * g + b


def _key_to_float(key):
    return lax.bitcast_convert_type(key ^ ((key >> 31) & 0x7FFFFFFF), F32)


def _kth_largest_key(count_ge, shape, n_sel):
    zero = jnp.zeros(shape, jnp.int32)
    prefix = jnp.where(count_ge(zero) >= n_sel, zero, jnp.int32(-2 ** 31))

    def refine(step, prefix):
        cand = prefix | (jnp.int32(1 << 30) >> step)
        return jnp.where(count_ge(cand) >= n_sel, cand, prefix)

    return lax.fori_loop(0, 31, refine, prefix)


def _tie_break_block(score, tau, tied_f, running, need):
    w = score.shape[1]
    first = lax.broadcasted_iota(jnp.int32, (w, w), 0)
    second = lax.broadcasted_iota(jnp.int32, (w, w), 1)
    precedes = jnp.where(first < second, 1.0, 0.0).astype(BF16)
    rank = running + jnp.dot(tied_f.astype(BF16), precedes, preferred_element_type=F32)
    keep_tied = jnp.where(tied_f > 0.5, jnp.where(rank < need, 0.0, -jnp.inf), -jnp.inf)
    return jnp.where(score > tau, 0.0, keep_tied), running + jnp.sum(tied_f, axis=-1, keepdims=True)


def _ada_kernel(c_ref, w_ref, b_ref, o_ref):
    c = c_ref[...]
    s = (c * jax.nn.sigmoid(c)).astype(BF16)
    o_ref[0] = jnp.dot(s, w_ref[0].astype(BF16), preferred_element_type=F32) + b_ref[0]


def _ada(c_all, w_ada, b_ada):
    tn = 1024
    return pl.pallas_call(
        _ada_kernel,
        out_shape=jax.ShapeDtypeStruct((DEPTH, ADA_ROWS, ADA_WIDTH), F32),
        grid=(DEPTH, ADA_WIDTH // tn),
        in_specs=[
            pl.BlockSpec((ADA_ROWS, D_MODEL), lambda l, j: (0, 0)),
            pl.BlockSpec((1, D_MODEL, tn), lambda l, j: (l, 0, j)),
            pl.BlockSpec((1, 1, tn), lambda l, j: (l, 0, j)),
        ],
        out_specs=pl.BlockSpec((1, ADA_ROWS, tn), lambda l, j: (l, 0, j)),
        compiler_params=_params("parallel", "parallel"),
        name="ada",
    )(c_all, w_ada, b_ada.reshape(DEPTH, 1, ADA_WIDTH))


def _mod_kernel(x_ref, sh_ref, sc_ref, h_ref):
    h_ref[...] = (x_ref[...] * (1.0 + sc_ref[0]) + sh_ref[0]).astype(BF16)


def _mod_spec(mod, tiles_per_group):
    return pl.BlockSpec((1, mod.shape[1], D_MODEL), lambda i, *_: (i // tiles_per_group, 0, 0))


def _modulate(x, shift, scale, tm, tpg):
    m = x.shape[0]
    return pl.pallas_call(
        _mod_kernel,
        out_shape=jax.ShapeDtypeStruct((m, D_MODEL), BF16),
        grid=(m // tm,),
        in_specs=[pl.BlockSpec((tm, D_MODEL), lambda i: (i, 0)), _mod_spec(shift, tpg), _mod_spec(scale, tpg)],
        out_specs=pl.BlockSpec((tm, D_MODEL), lambda i: (i, 0)),
        compiler_params=_params("parallel"),
        name="modulate",
    )(x, shift, scale)


def _up_kernel(h_ref, wa_ref, wb_ref, o_ref, wa_s, wb_s):
    @pl.when(pl.program_id(1) == 0)
    def _():
        wa_s[...] = wa_ref[0, 0].astype(BF16)
        wb_s[...] = wb_ref[0, 0].astype(BF16)

    h = h_ref[...]
    a = jnp.dot(h, wa_s[...], preferred_element_type=F32)
    b = jnp.dot(h, wb_s[...], preferred_element_type=F32)
    o_ref[...] = (a * jax.nn.sigmoid(a) * b).astype(BF16)


def _ffn_up(h, w_up, l, s, tm):
    m = h.shape[0]
    tf = 512
    nf = D_FF // tf
    return pl.pallas_call(
        _up_kernel,
        out_shape=jax.ShapeDtypeStruct((m, D_FF), BF16),
        grid=(nf, m // tm),
        in_specs=[
            pl.BlockSpec((tm, D_MODEL), lambda j, i: (i, 0)),
            pl.BlockSpec((1, 1, D_MODEL, tf), lambda j, i: (l, s, 0, j)),
            pl.BlockSpec((1, 1, D_MODEL, tf), lambda j, i: (l, s, 0, j + nf)),
        ],
        out_specs=pl.BlockSpec((tm, tf), lambda j, i: (i, j)),
        scratch_shapes=[pltpu.VMEM((D_MODEL, tf), BF16), pltpu.VMEM((D_MODEL, tf), BF16)],
        compiler_params=_params("parallel", "arbitrary"),
        name="ffn_up",
    )(h, w_up, w_up)


def _down_kernel(*refs, gscale, emit_h):
    if emit_h:
        a_ref, w_ref, x_ref, g_ref, lg_ref, lb_ref, sh_ref, sc_ref, xo_ref, ho_ref = refs
    else:
        a_ref, w_ref, x_ref, g_ref, lg_ref, lb_ref, xo_ref = refs
    y = jnp.dot(a_ref[...], w_ref[0, 0], preferred_element_type=F32)
    r = ALPHA * x_ref[...] + (gscale * g_ref[0]) * y
    xn = _layer_norm_rows(r, lg_ref[...], lb_ref[...])
    xo_ref[...] = xn
    if emit_h:
        ho_ref[...] = (xn * (1.0 + sc_ref[0]) + sh_ref[0]).astype(BF16)


def _proj_resid_ln(a, w, l, s, x, gate, lg, lb, nxt, gscale, tm, rows_per_group):
    m, kdim = a.shape
    tm = min(tm, DOWN_ROWS) if kdim == D_FF else tm
    tpg = max(rows_per_group // tm, 1)
    emit_h = nxt is not None
    row = lambda i: (i, 0)
    const = lambda i: (0, 0)
    in_specs = [
        pl.BlockSpec((tm, kdim), row),
        pl.BlockSpec((1, 1, kdim, D_MODEL), lambda i: (l, s, 0, 0), pipeline_mode=pl.Buffered(1)),
        pl.BlockSpec((tm, D_MODEL), row),
        _mod_spec(gate, tpg),
        pl.BlockSpec((1, D_MODEL), const),
        pl.BlockSpec((1, D_MODEL), const),
    ]
    args = [a, w, x, gate, lg, lb]
    out_shape = [jax.ShapeDtypeStruct((m, D_MODEL), F32)]
    out_specs = [pl.BlockSpec((tm, D_MODEL), row)]
    if emit_h:
        in_specs += [_mod_spec(nxt[0], tpg), _mod_spec(nxt[1], tpg)]
        args += [nxt[0], nxt[1]]
        out_shape.append(jax.ShapeDtypeStruct((m, D_MODEL), BF16))
        out_specs.append(pl.BlockSpec((tm, D_MODEL), row))
    outs = pl.pallas_call(
        functools.partial(_down_kernel, gscale=gscale, emit_h=emit_h),
        out_shape=out_shape,
        grid=(m // tm,),
        in_specs=in_specs,
        out_specs=out_specs,
        compiler_params=_params("arbitrary"),
        name="proj_resid_ln",
    )(*args)
    return (outs[0], outs[1]) if emit_h else (outs[0], None)


def _in_proj_kernel(h_ref, w_ref, o_ref, w_s):
    @pl.when(pl.program_id(1) == 0)
    def _():
        w_s[...] = w_ref[0].T.astype(BF16)

    o_ref[...] = jnp.dot(h_ref[...], w_s[...], preferred_element_type=F32)


def _in_proj(h, w_in_t, l, tm):
    m = h.shape[0]
    return pl.pallas_call(
        _in_proj_kernel,
        out_shape=jax.ShapeDtypeStruct((m, IN_MAIN_WIDTH), F32),
        grid=(IN_MAIN_WIDTH // IN_TN, m // tm),
        in_specs=[pl.BlockSpec((tm, D_MODEL), lambda j, i: (i, 0)),
                  pl.BlockSpec((1, IN_TN, D_MODEL), lambda j, i: (l, j, 0))],
        out_specs=pl.BlockSpec((tm, IN_TN), lambda j, i: (i, j)),
        scratch_shapes=[pltpu.VMEM((D_MODEL, IN_TN), BF16)],
        compiler_params=_params("parallel", "arbitrary"),
        name="in_proj",
    )(h, w_in_t)


def _gmlp_kernel(*refs, tr, chunk, sample):
    (u_ref, sv_ref, k_ref, v_ref, ki_ref, sg_ref, sb_ref, ws_ref, bias_ref, kg_ref, kb_ref,
     oa_ref, k4_ref, v4_ref, kbf_ref, kiln_ref, kibf_ref, vx_ref) = refs[:18]
    u = jax.nn.gelu(u_ref[...])
    svn = _layer_norm_rows(jax.nn.gelu(sv_ref[...]), sg_ref[...], sb_ref[...])
    if sample:
        refs[18][...] = svn
    svb = svn.astype(BF16)
    row = lax.broadcasted_iota(jnp.int32, (tr, tr), 0)
    col = lax.broadcasted_iota(jnp.int32, (tr, tr), 1)
    shift = chunk.bit_length() - 1
    visible = ((row >> shift) == (col >> shift)) & (col <= row)
    for g in range(GMLP_GROUPS):
        cols = slice(g * GMLP_GDIM, (g + 1) * GMLP_GDIM)
        wm = jnp.where(visible, ws_ref[g], 0.0).astype(BF16)
        f = jnp.dot(wm, svb[:, cols], preferred_element_type=F32) + bias_ref[:, cols]
        oa_ref[:, cols] = (u[:, cols] * f).astype(BF16)
    k = k_ref[...]
    v = v_ref[...]
    for h in range(N_KV_HEADS):
        k4_ref[:, h, :] = k[:, h * HEAD_DIM:(h + 1) * HEAD_DIM]
        v4_ref[:, h, :] = v[:, h * HEAD_DIM:(h + 1) * HEAD_DIM]
    kbf_ref[...] = k.astype(BF16)
    vx_ref[...] = v.astype(BF16)
    kiln = _layer_norm_rows(ki_ref[...], kg_ref[...], kb_ref[...])
    kiln_ref[...] = kiln
    kibf_ref[...] = kiln.astype(BF16)


def _gmlp(p, sgu_g, sgu_b, w_tiled, bias_tiled, ki_g, ki_b, chunk, tr, sample):
    m = p.shape[0]
    row = lambda c: (lambda i: (i, c))
    const2 = lambda i: (0, 0)
    out_shape = [
        jax.ShapeDtypeStruct((m, D_A), BF16),
        jax.ShapeDtypeStruct((m, N_KV_HEADS, HEAD_DIM), F32),
        jax.ShapeDtypeStruct((m, N_KV_HEADS, HEAD_DIM), F32),
        jax.ShapeDtypeStruct((m, KV_WIDTH), BF16),
        jax.ShapeDtypeStruct((m, IDX_DIM), F32),
        jax.ShapeDtypeStruct((m, IDX_DIM), BF16),
        jax.ShapeDtypeStruct((m, KV_WIDTH), BF16),
    ]
    head4 = pl.BlockSpec((tr, N_KV_HEADS, HEAD_DIM), lambda i: (i, 0, 0))
    out_specs = [
        pl.BlockSpec((tr, D_A), row(0)), head4, head4,
        pl.BlockSpec((tr, KV_WIDTH), row(0)),
        pl.BlockSpec((tr, IDX_DIM), row(0)),
        pl.BlockSpec((tr, IDX_DIM), row(0)),
        pl.BlockSpec((tr, KV_WIDTH), row(0)),
    ]
    if sample:
        out_shape.append(jax.ShapeDtypeStruct((m, D_A), F32))
        out_specs.append(pl.BlockSpec((tr, D_A), row(0)))
    return pl.pallas_call(
        functools.partial(_gmlp_kernel, tr=tr, chunk=chunk, sample=sample),
        out_shape=out_shape,
        grid=(m // tr,),
        in_specs=[
            pl.BlockSpec((tr, D_A), row(COL_U // D_A)),
            pl.BlockSpec((tr, D_A), row(COL_SV // D_A)),
            pl.BlockSpec((tr, KV_WIDTH), row(COL_K // KV_WIDTH)),
            pl.BlockSpec((tr, KV_WIDTH), row(COL_V // KV_WIDTH)),
            pl.BlockSpec((tr, IDX_DIM), row(COL_KI // IDX_DIM)),
            pl.BlockSpec((1, D_A), const2),
            pl.BlockSpec((1, D_A), const2),
            pl.BlockSpec((GMLP_GROUPS, tr, tr), lambda i: (0, 0, 0)),
            pl.BlockSpec((tr, D_A), const2),
            pl.BlockSpec((1, IDX_DIM), const2),
            pl.BlockSpec((1, IDX_DIM), const2),
        ],
        out_specs=out_specs,
        compiler_params=_params("parallel"),
        name="gmlp",
    )(p, p, p, p, p, sgu_g, sgu_b, w_tiled, bias_tiled, ki_g, ki_b)


def _dsa_prompt_body(i, q_ref, qi0_ref, qi1_ref, wi_ref, k_ref, v_ref, ki_ref, o_ref, score_ref, score_t_ref,
                     bias_ref, *, tq, n_keys, n_sel):
    half = IDX_HEADS // 2
    wi = wi_ref[...]
    qi_rows = jnp.concatenate(
        [(qi0_ref if h < half else qi1_ref)[:, (h % half) * IDX_DIM:(h % half + 1) * IDX_DIM]
         for h in range(IDX_HEADS)], axis=0).astype(BF16)
    for c0 in range(0, n_keys, IDX_KEY_CHUNK):
        c1 = min(c0 + IDX_KEY_CHUNK, n_keys)
        logits = lax.dot_general(qi_rows, ki_ref[c0:c1, :], NT_DIMS, preferred_element_type=F32)
        total = jnp.maximum(logits[0:tq], 0.0) * wi[:, 0:1]
        for h in range(1, IDX_HEADS):
            total = total + jnp.maximum(logits[h * tq:(h + 1) * tq], 0.0) * wi[:, h:h + 1]
        score_ref[:, c0:c1] = total * IDX_SCALE
    qpos = i * tq + lax.broadcasted_iota(jnp.int32, (tq, n_keys), 0)
    kpos = lax.broadcasted_iota(jnp.int32, (tq, n_keys), 1)
    admissible = (kpos >> CHUNK_SHIFT) <= (qpos >> CHUNK_SHIFT)
    score_ref[:, 0:n_keys] = jnp.where(admissible, score_ref[:, 0:n_keys], -jnp.inf)
    n_adm = jnp.sum(jnp.where(admissible, 1.0, 0.0), axis=-1, keepdims=True)

    for c0 in range(0, n_keys, tq):
        score_t_ref[c0:c0 + tq, :] = score_ref[:, c0:c0 + tq].T

    def count_ge(key):
        cf = _key_to_float(key)
        part = jnp.where(score_t_ref[0:tq, :] >= cf, 1.0, 0.0)
        for r0 in range(tq, n_keys, tq):
            part = part + jnp.where(score_t_ref[r0:r0 + tq, :] >= cf, 1.0, 0.0)
        return jnp.sum(part, axis=0, keepdims=True)

    prefix = _kth_largest_key(count_ge, (1, tq), n_sel)
    tau_col = jnp.broadcast_to(_key_to_float(prefix), (tq, tq)).T[:, 0:1]
    tau = jnp.where(n_adm <= n_sel, -jnp.inf, tau_col)
    need = n_sel - jnp.sum(jnp.where(score_ref[:, 0:n_keys] > tau, 1.0, 0.0), axis=-1, keepdims=True)
    running = jnp.zeros((tq, 1), F32)
    q_chunk = (i * tq + lax.broadcasted_iota(jnp.int32, (tq, tq), 0)) >> CHUNK_SHIFT
    for c0 in range(0, n_keys, tq):
        blk = score_ref[:, c0:c0 + tq]
        adm = ((c0 + lax.broadcasted_iota(jnp.int32, (tq, tq), 1)) >> CHUNK_SHIFT) <= q_chunk
        tied_f = jnp.where(adm, jnp.where(blk == tau, 1.0, 0.0), 0.0)
        bias_ref[:, c0:c0 + tq], running = _tie_break_block(blk, tau, tied_f, running, need)

    bias = bias_ref[:, 0:n_keys]

    for hk in range(N_KV_HEADS):
        heads = [hk * Q_PER_KV + g for g in range(Q_PER_KV)]
        cols = slice(hk * HEAD_DIM, (hk + 1) * HEAD_DIM)
        qs = jnp.concatenate([q_ref[:, h * HEAD_DIM:(h + 1) * HEAD_DIM] for h in heads], axis=0)
        qs = (qs * (ATTN_SCALE * LOG2_E)).astype(BF16)
        s = lax.dot_general(qs, k_ref[0:n_keys, cols], NT_DIMS, preferred_element_type=F32)
        ps, ls = [], []
        for g in range(Q_PER_KV):
            sg = s[g * tq:(g + 1) * tq] + bias
            pg = jnp.exp2(sg - jnp.max(sg, axis=-1, keepdims=True))
            ls.append(jnp.sum(pg, axis=-1, keepdims=True))
            ps.append(pg.astype(BF16))
        o = jnp.dot(jnp.concatenate(ps, axis=0), v_ref[0:n_keys, cols], preferred_element_type=F32)
        for g, h in enumerate(heads):
            o_ref[:, h * HEAD_DIM:(h + 1) * HEAD_DIM] = (o[g * tq:(g + 1) * tq] / ls[g]).astype(BF16)


def _dsa_prompt_kernel(*refs, tq, nq, n_sel, n_buckets):
    i = pl.program_id(1)
    per = nq // n_buckets
    for c in range(n_buckets):
        @pl.when((i >= c * per) & (i < (c + 1) * per))
        def _(c=c):
            _dsa_prompt_body(i, *refs, tq=tq, n_keys=(c + 1) * per * tq, n_sel=n_sel)


def _dsa_prompt(p, kbf, vbf, kibf, nb, t):
    tq = 128
    nq = t // tq
    n_sel = min(TOPK_MAX, t // 4)
    qrow = lambda c: (lambda b, i: (b * nq + i, c))
    return pl.pallas_call(
        functools.partial(_dsa_prompt_kernel, tq=tq, nq=nq, n_sel=n_sel, n_buckets=DSA_BUCKETS),
        out_shape=jax.ShapeDtypeStruct((nb * t, D_B), BF16),
        grid=(nb, nq),
        in_specs=[
            pl.BlockSpec((tq, D_B), qrow(COL_Q // D_B)),
            pl.BlockSpec((tq, IDX_WIDTH // 2), qrow(2 * COL_QI // IDX_WIDTH)),
            pl.BlockSpec((tq, IDX_WIDTH // 2), qrow(2 * COL_QI // IDX_WIDTH + 1)),
            pl.BlockSpec((tq, LANE), qrow(COL_WI // LANE)),
            pl.BlockSpec((t, KV_WIDTH), lambda b, i: (b, 0)),
            pl.BlockSpec((t, KV_WIDTH), lambda b, i: (b, 0)),
            pl.BlockSpec((t, IDX_DIM), lambda b, i: (b, 0)),
        ],
        out_specs=pl.BlockSpec((tq, D_B), lambda b, i: (b * nq + i, 0)),
        scratch_shapes=[pltpu.VMEM((tq, t), F32), pltpu.VMEM((t, tq), F32), pltpu.VMEM((tq, t), F32)],
        compiler_params=_params("parallel", "parallel"),
        name="dsa_prompt",
    )(p, p, p, p, kbf, vbf, kibf)


def _dsa_sample_kernel(q_ref, qi0_ref, qi1_ref, wi_ref, kn_ref, vn_ref, kin_ref, kp_ref, vp_ref, kip_ref,
                       o_ref, kp_s, vp_s, spc_ref, sp_ref, bp_ref, bn_ref, *, tq, n_chunks, chunk_len, n_sel):
    c = pl.program_id(1)
    off = pl.multiple_of(c * chunk_len, chunk_len)
    for h in range(N_KV_HEADS):
        kp_s[h, pl.ds(off, chunk_len), :] = kp_ref[0, 0, pl.ds(h, chunk_len, stride=N_KV_HEADS), :].astype(BF16)
        vp_s[h, pl.ds(off, chunk_len), :] = vp_ref[0, 0, pl.ds(h, chunk_len, stride=N_KV_HEADS), :].astype(BF16)

    half = IDX_HEADS // 2
    wi = wi_ref[...]
    qi_rows = jnp.concatenate(
        [(qi0_ref if h < half else qi1_ref)[:, (h % half) * IDX_DIM:(h % half + 1) * IDX_DIM]
         for h in range(IDX_HEADS)], axis=0).astype(BF16)

    def index_scores(keys):
        logits = lax.dot_general(qi_rows, keys, NT_DIMS, preferred_element_type=F32)
        total = jnp.maximum(logits[0:tq], 0.0) * wi[:, 0:1]
        for h in range(1, IDX_HEADS):
            total = total + jnp.maximum(logits[h * tq:(h + 1) * tq], 0.0) * wi[:, h:h + 1]
        return total * IDX_SCALE

    spc_ref[c] = index_scores(kip_ref[0, 0].astype(BF16))

    @pl.when(c == n_chunks - 1)
    def _():
        for j in range(n_chunks):
            sp_ref[:, j * chunk_len:(j + 1) * chunk_len] = spc_ref[j]
        s_new = index_scores(kin_ref[...])

        def count_ge_float(cf):
            return (jnp.sum(jnp.where(sp_ref[...] >= cf, 1.0, 0.0), axis=-1, keepdims=True)
                    + jnp.sum(jnp.where(s_new >= cf, 1.0, 0.0), axis=-1, keepdims=True))

        def count_ge(key):
            return count_ge_float(_key_to_float(key))

        tau = _key_to_float(_kth_largest_key(count_ge, (tq, 1), n_sel))
        bp_ref[...] = jnp.where(sp_ref[...] >= tau, 0.0, -jnp.inf)
        bn_ref[...] = jnp.where(s_new >= tau, 0.0, -jnp.inf)

        @pl.when(jnp.max(count_ge_float(tau)) > n_sel)
        def _():
            need = n_sel - (jnp.sum(jnp.where(sp_ref[...] > tau, 1.0, 0.0), axis=-1, keepdims=True)
                            + jnp.sum(jnp.where(s_new > tau, 1.0, 0.0), axis=-1, keepdims=True))
            running = jnp.zeros((tq, 1), F32)
            for c0 in range(0, PAST_LEN, LANE):
                blk = sp_ref[:, c0:c0 + LANE]
                bp_ref[:, c0:c0 + LANE], running = _tie_break_block(
                    blk, tau, jnp.where(blk == tau, 1.0, 0.0), running, need)
            bn_ref[...], _ = _tie_break_block(s_new, tau, jnp.where(s_new == tau, 1.0, 0.0), running, need)

        bias_p = bp_ref[...]
        bias_n = bn_ref[...]

        for hk in range(N_KV_HEADS):
            heads = [hk * Q_PER_KV + g for g in range(Q_PER_KV)]
            cols = slice(hk * HEAD_DIM, (hk + 1) * HEAD_DIM)
            qs = jnp.concatenate([q_ref[:, h * HEAD_DIM:(h + 1) * HEAD_DIM] for h in heads], axis=0).astype(BF16)
            s_p = lax.dot_general(qs, kp_s[hk], NT_DIMS, preferred_element_type=F32) * ATTN_SCALE
            s_n = lax.dot_general(qs, kn_ref[:, cols], NT_DIMS, preferred_element_type=F32) * ATTN_SCALE
            pps, pns, ls = [], [], []
            for g in range(Q_PER_KV):
                rows = slice(g * tq, (g + 1) * tq)
                sp_g = s_p[rows] + bias_p
                sn_g = s_n[rows] + bias_n
                mx = jnp.maximum(jnp.max(sp_g, axis=-1, keepdims=True), jnp.max(sn_g, axis=-1, keepdims=True))
                pp = jnp.exp(sp_g - mx)
                pn = jnp.exp(sn_g - mx)
                ls.append(jnp.sum(pp, axis=-1, keepdims=True) + jnp.sum(pn, axis=-1, keepdims=True))
                pps.append(pp.astype(BF16))
                pns.append(pn.astype(BF16))
            o = (jnp.dot(jnp.concatenate(pps, axis=0), vp_s[hk], preferred_element_type=F32)
                 + jnp.dot(jnp.concatenate(pns, axis=0), vn_ref[:, cols], preferred_element_type=F32))
            for g, h in enumerate(heads):
                o_ref[:, h * HEAD_DIM:(h + 1) * HEAD_DIM] = (o[g * tq:(g + 1) * tq] / ls[g]).astype(BF16)


def _dsa_sample(p, kbf, vbf, kibf, cache_k, cache_v, cache_kidx, l, nb, t):
    assert PAST_LEN % (1 << CHUNK_SHIFT) == 0 and t <= (1 << CHUNK_SHIFT)
    n_chunks = 4
    chunk_len = PAST_LEN // n_chunks
    n_sel = min(TOPK_MAX, (PAST_LEN + t) // 4)
    qrow = lambda c: (lambda b, j: (b, c))
    new = lambda b, j: (b, 0)
    past5 = pl.BlockSpec((1, 1, chunk_len * N_KV_HEADS, HEAD_DIM), lambda b, j: (l, b, j, 0))
    rows_per_batch = PAST_LEN * N_KV_HEADS
    cache_k = cache_k.reshape(DEPTH, nb, rows_per_batch, HEAD_DIM)
    cache_v = cache_v.reshape(DEPTH, nb, rows_per_batch, HEAD_DIM)
    return pl.pallas_call(
        functools.partial(_dsa_sample_kernel, tq=t, n_chunks=n_chunks, chunk_len=chunk_len, n_sel=n_sel),
        out_shape=jax.ShapeDtypeStruct((nb * t, D_B), BF16),
        grid=(nb, n_chunks),
        in_specs=[
            pl.BlockSpec((t, D_B), qrow(COL_Q // D_B)),
            pl.BlockSpec((t, IDX_WIDTH // 2), qrow(2 * COL_QI // IDX_WIDTH)),
            pl.BlockSpec((t, IDX_WIDTH // 2), qrow(2 * COL_QI // IDX_WIDTH + 1)),
            pl.BlockSpec((t, LANE), qrow(COL_WI // LANE)),
            pl.BlockSpec((t, KV_WIDTH), new),
            pl.BlockSpec((t, KV_WIDTH), new),
            pl.BlockSpec((t, IDX_DIM), new),
            past5, past5,
            pl.BlockSpec((1, 1, chunk_len, IDX_DIM), lambda b, j: (l, b, j, 0)),
        ],
        out_specs=pl.BlockSpec((t, D_B), new),
        scratch_shapes=[pltpu.VMEM((N_KV_HEADS, PAST_LEN, HEAD_DIM), BF16),
                        pltpu.VMEM((N_KV_HEADS, PAST_LEN, HEAD_DIM), BF16),
                        pltpu.VMEM((n_chunks, t, chunk_len), F32),
                        pltpu.VMEM((t, PAST_LEN), F32),
                        pltpu.VMEM((t, PAST_LEN), F32),
                        pltpu.VMEM((t, t), F32)],
        compiler_params=_params("parallel", "arbitrary"),
        name="dsa_sample",
    )(p, p, p, p, kbf, vbf, kibf, cache_k, cache_v, cache_kidx)


def _merge_kernel(h_ref, a_ref, b_ref, wga_ref, wgb_ref, wa_ref, wb_ref, o_ref):
    h = h_ref[...]
    ga = jax.nn.sigmoid(lax.dot_general(h, wga_ref[0], NT_DIMS, preferred_element_type=F32))
    gb = jax.nn.sigmoid(lax.dot_general(h, wgb_ref[0], NT_DIMS, preferred_element_type=F32))
    ya = jnp.dot(a_ref[...], wa_ref[0], preferred_element_type=F32)
    yb = jnp.dot(b_ref[...], wb_ref[0], preferred_element_type=F32)
    o_ref[...] = (ga * ya + gb * yb).astype(BF16)


def _merge(h, out_a, out_b, w_gates, w_pa, w_pb, l, tm):
    m = h.shape[0]
    tn = 512
    nn = D_MODEL // tn
    rows = pl.BlockSpec((tm, D_MODEL), lambda j, i: (i, 0))
    wspec = lambda off: pl.BlockSpec((1, D_MODEL, tn), lambda j, i: (l, 0, j + off))
    gspec = lambda off: pl.BlockSpec((1, tn, D_MODEL), lambda j, i: (l, j + off, 0))
    return pl.pallas_call(
        _merge_kernel,
        out_shape=jax.ShapeDtypeStruct((m, D_MODEL), BF16),
        grid=(nn, m // tm),
        in_specs=[rows, rows, rows, gspec(0), gspec(nn), wspec(0), wspec(0)],
        out_specs=pl.BlockSpec((tm, tn), lambda j, i: (i, j)),
        compiler_params=_params("parallel", "parallel"),
        name="merge",
    )(h, out_a, out_b, w_gates, w_gates, w_pa, w_pb)


def _run_trunk(x, mods, wts, nb, t, tm, tr, chunk, caches):
    sample = caches is not None
    tm_wide = min(2 * tm, x.shape[0])
    tpg = max(t // tm, 1)
    h = _modulate(x, mods(0, 0, 0), mods(0, 0, 1), tm, tpg)
    new_k, new_v, new_ki, new_sv = [], [], [], []
    for l in range(DEPTH):
        ln = lambda s: (wts["ln_g"][l, s][None], wts["ln_b"][l, s][None])
        act = _ffn_up(h, wts["up"], l, 0, tm_wide)
        x, h = _proj_resid_ln(act, wts["down"], l, 0, x, mods(l, 0, 2), *ln(0),
                              (mods(l, 1, 0), mods(l, 1, 1)), 0.5, tm, t)
        p = _in_proj(h, wts["in"], l, tm_wide)
        outs = _gmlp(p, wts["sgu_g"][l][None], wts["sgu_b"][l][None], *wts["spatial"][l][chunk],
                     wts["ki_g"][l][None], wts["ki_b"][l][None], chunk, tr, sample)
        out_a, k4, v4, kbf, kiln, kibf, vx = outs[:7]
        if sample:
            out_b = _dsa_sample(p, kbf, vx, kibf, *caches, l, nb, t)
            new_sv.append(outs[7].reshape(nb, t, D_A))
        else:
            out_b = _dsa_prompt(p, kbf, vx, kibf, nb, t)
        z = _merge(h, out_a, out_b, wts["gates"], wts["pa"], wts["pb"], l, tm)
        x, h = _proj_resid_ln(z, wts["out"], l, 0, x, mods(l, 1, 2), *ln(1),
                              (mods(l, 2, 0), mods(l, 2, 1)), 1.0, tm, t)
        act = _ffn_up(h, wts["up"], l, 1, tm_wide)
        nxt = (mods(l + 1, 0, 0), mods(l + 1, 0, 1)) if l + 1 < DEPTH else None
        x, h = _proj_resid_ln(act, wts["down"], l, 1, x, mods(l, 2, 2), *ln(2), nxt, 0.5, tm, t)
        new_k.append(k4.reshape(nb, t, N_KV_HEADS, HEAD_DIM))
        new_v.append(v4.reshape(nb, t, N_KV_HEADS, HEAD_DIM))
        new_ki.append(kiln.reshape(nb, t, IDX_DIM))
    return x, jnp.stack(new_k), jnp.stack(new_v), jnp.stack(new_ki), (jnp.stack(new_sv) if sample else None)


def kernel(x_prompt, x_sample, c_prompt, c_sample, cache_k, cache_v, cache_kidx, w_ada, b_ada, w_ffn_up,
           w_ffn_down, w_in, sgu_ln_g, sgu_ln_b, w_spatial, b_spatial, kidx_ln_g, kidx_ln_b, w_branch_a,
           w_branch_b, w_out, ln_g, ln_b):
    tr = 256
    c_all = jnp.concatenate([c_prompt, c_sample, jnp.zeros((ADA_ROWS - BATCH - DEC_BATCH, D_MODEL), F32)], axis=0)
    ada = _ada(c_all, w_ada, b_ada)

    def mod_cols(l, sub, kind, rows):
        c0 = (sub * 3 + kind) * D_MODEL
        return ada[l, rows, c0:c0 + D_MODEL]

    def mods_prompt(l, sub, kind):
        return mod_cols(l, sub, kind, slice(0, BATCH))[:, None, :]

    def mods_sample(l, sub, kind):
        rows = mod_cols(l, sub, kind, slice(BATCH, BATCH + DEC_BATCH))
        return jnp.repeat(rows, DEC_SEQ, axis=0)[None]

    def spatial(l, chunk):
        reps = tr // chunk
        w = jnp.tile(w_spatial[l][:, :chunk, :chunk], (1, reps, reps))
        b = jnp.tile(jnp.repeat(jnp.transpose(b_spatial[l][:, :chunk]), GMLP_GDIM, axis=1), (reps, 1))
        return w, b

    w_in_t = jnp.swapaxes(w_in, 1, 2)
    wts = dict(
        up=w_ffn_up, **{"in": w_in_t},
        down=w_ffn_down.astype(BF16),
        out=w_out.astype(BF16)[:, None],
        pa=w_branch_a.astype(BF16), pb=w_branch_b.astype(BF16),
        gates=w_in_t[:, COL_GATES:COL_GATES + 2 * D_MODEL, :].astype(BF16),
        ln_g=ln_g, ln_b=ln_b, sgu_g=sgu_ln_g, sgu_b=sgu_ln_b, ki_g=kidx_ln_g, ki_b=kidx_ln_b,
        spatial=[{c: spatial(l, c) for c in (GMLP_CHUNK, DEC_SEQ)} for l in range(DEPTH)],
    )

    y_p, k_p, v_p, ki_p, _ = _run_trunk(x_prompt.reshape(BATCH * SEQ, D_MODEL), mods_prompt, wts,
                                        BATCH, SEQ, 512, tr, GMLP_CHUNK, None)
    m_s = DEC_BATCH * DEC_SEQ
    y_s, k_s, v_s, ki_s, sv_s = _run_trunk(x_sample.reshape(m_s, D_MODEL), mods_sample, wts,
                                           DEC_BATCH, DEC_SEQ, m_s, tr, DEC_SEQ, (cache_k, cache_v, cache_kidx))
    return (y_p.reshape(BATCH, SEQ, D_MODEL), y_s.reshape(DEC_BATCH, DEC_SEQ, D_MODEL),
            k_p, v_p, ki_p, k_s, v_s, ki_s, sv_s)
```
